```python
import math
import jax, jax.numpy as jnp
from jax import lax
import numpy as np

D_MODEL = 1024
BATCH = 8
SEQ = 2048
DEPTH = 4
DEC_BATCH = 128
DEC_SEQ = 8
PAST_LEN = 16384
PAGE_SIZE = 128

N_META = 16
D_CONV = D_MODEL
CONV_K = 31
D_INNER = 2 * D_MODEL
N_HEADS = 4
HEAD_DIM = D_INNER // N_HEADS
QKV_BLOCK = 4
MCONV_K = 4
CHUNK = 64
N_EXPERTS = 64
TOP_K = 8
N_GROUPS = 8
TOPK_GROUPS = 4
D_EXPERT = D_MODEL // 4
D_SHARED = D_EXPERT
ROUTE_SCALE = 2.5
EXPERT_BLOCK = 128
ALPHA = (2 * DEPTH) ** 0.25
BETA = (8 * DEPTH) ** -0.25
LN_EPS = 1e-5
IN_SPLITS = [D_CONV, 2 * D_CONV, 2 * D_CONV + D_INNER, 2 * D_CONV + 2 * D_INNER, 2 * D_CONV + 2 * D_INNER + D_MODEL]
N_IN = 2 * D_CONV + 2 * D_INNER + 2 * D_MODEL

kernel_name = 'conformer_mlstm_gated_moe_deepnorm_step'


def layer_norm(x, g, b=None):
    xf = x.astype(jnp.float32)
    mu = jnp.mean(xf, axis=-1, keepdims=True)
    var = jnp.mean(jnp.square(xf - mu), axis=-1, keepdims=True)
    y = (xf - mu) * lax.rsqrt(var + LN_EPS) * g.astype(jnp.float32)
    if b is not None:
        y = y + b.astype(jnp.float32)
    return y.astype(x.dtype)


def causal_dwconv(u, buf, w, b):
    full = jnp.concatenate([buf.astype(u.dtype), u], axis=1)
    out = lax.conv_general_dilated(full, w[:, None, :].astype(u.dtype), window_strides=(1,), padding='VALID',
                                   dimension_numbers=('NWC', 'WIO', 'NWC'), feature_group_count=u.shape[-1])
    return out + b.astype(u.dtype), full[:, -(w.shape[0] - 1):]


def headwise(x, w):
    B, L, _ = x.shape
    xb = x.reshape(B, L, w.shape[0], w.shape[1])
    return jnp.einsum('blni,nio->blno', xb, w.astype(x.dtype)).reshape(B, L, -1)


def mlstm_chunk(state, inp):
    C0, n0, m0 = state
    q, k, v, ig, lf = inp
    L = q.shape[2]
    b = jnp.cumsum(lf, axis=-1)
    causal = jnp.tril(jnp.ones((L, L), dtype=bool))
    log_w = jnp.where(causal, b[..., :, None] - b[..., None, :] + ig[..., None, :], -jnp.inf)
    log_s = b + m0[..., None]
    m = jnp.maximum(log_s, jnp.max(log_w, axis=-1))
    w = jnp.exp(log_w - m[..., None])
    s = jnp.exp(log_s - m)
    qk = jnp.einsum('bhtd,bhsd->bhts', q, k) * w
    num = s[..., None] * jnp.einsum('bhtd,bhde->bhte', q, C0) + jnp.einsum('bhts,bhse->bhte', qk, v)
    den = s * jnp.einsum('bhtd,bhd->bht', q, n0) + jnp.sum(qk, axis=-1)
    h = num / jnp.maximum(jnp.abs(den), jnp.exp(-m))[..., None]
    m_end = m[..., -1]
    w_end = jnp.exp(b[..., -1:] - b + ig - m_end[..., None])
    s_end = jnp.exp(b[..., -1] + m0 - m_end)
    C1 = s_end[..., None, None] * C0 + jnp.einsum('bhsd,bhse->bhde', k * w_end[..., None], v)
    n1 = s_end[..., None] * n0 + jnp.einsum('bhs,bhsd->bhd', w_end, k)
    return (C1, n1, m_end), h


def mlstm_sequence(q, k, v, ig, lf, state, lead, chunk):
    parts = []
    if lead > 0:
        state, h0 = mlstm_chunk(state, (q[:, :, :lead], k[:, :, :lead], v[:, :, :lead], ig[..., :lead], lf[..., :lead]))
        parts.append(h0)
    rest = q.shape[2] - lead
    nc = rest // chunk

    def to_chunks(a):
        a = a[:, :, lead:]
        a = a.reshape(a.shape[:2] + (nc, chunk) + a.shape[3:])
        return jnp.moveaxis(a, 2, 0)

    state, hr = lax.scan(mlstm_chunk, state, tuple(to_chunks(a) for a in (q, k, v, ig, lf)))
    hr = jnp.moveaxis(hr, 0, 2).reshape(q.shape[:2] + (rest, v.shape[-1]))
    parts.append(hr)
    return state, jnp.concatenate(parts, axis=2)


def mix_sublayer(h, conv_buf, mconv_buf, C0, n0, m0, lead, chunk,
                 w_in, conv_w, conv_b, conv_ln_g, conv_ln_b, w_conv_out, mconv_w, mconv_b,
                 w_q, w_k, w_v, w_if, b_if, mh_ln_g, skip, w_m_down, w_o):
    B, L, _ = h.shape
    f32 = jnp.float32
    proj = h @ w_in.astype(h.dtype)
    glu_a, glu_b, xm, og, ga, gb = jnp.split(proj, IN_SPLITS, axis=-1)
    u = glu_a * jax.nn.sigmoid(glu_b)
    ca, conv_buf_new = causal_dwconv(u, conv_buf, conv_w, conv_b)
    ya = jax.nn.silu(layer_norm(ca, conv_ln_g, conv_ln_b)) @ w_conv_out.astype(h.dtype)
    cm, mconv_buf_new = causal_dwconv(xm, mconv_buf, mconv_w, mconv_b)
    cm = jax.nn.silu(cm)
    q = headwise(cm, w_q)
    k = headwise(cm, w_k)
    v = headwise(xm, w_v)
    gates = jnp.concatenate([q, k, v], axis=-1).astype(f32) @ w_if.astype(f32) + b_if.astype(f32)
    ig = jnp.transpose(gates[..., :N_HEADS], (0, 2, 1))
    lf = jnp.transpose(jax.nn.log_sigmoid(gates[..., N_HEADS:]), (0, 2, 1))
    to_heads = lambda a: jnp.transpose(a.astype(f32).reshape(B, L, N_HEADS, HEAD_DIM), (0, 2, 1, 3))
    qh = to_heads(q)
    kh = to_heads(k) * (HEAD_DIM ** -0.5)
    vh = to_heads(v)
    state0 = (C0.astype(f32), n0.astype(f32), m0.astype(f32))
    (C1, n1, m1), hh = mlstm_sequence(qh, kh, vh, ig, lf, state0, lead, chunk)
    mu = jnp.mean(hh, axis=-1, keepdims=True)
    var = jnp.mean(jnp.square(hh - mu), axis=-1, keepdims=True)
    hh = (hh - mu) * lax.rsqrt(var + LN_EPS)
    hh = jnp.transpose(hh, (0, 2, 1, 3)).reshape(B, L, D_INNER) * mh_ln_g.astype(f32)
    hb = (hh.astype(h.dtype) + skip.astype(h.dtype) * cm) * jax.nn.sigmoid(og)
    yb = hb @ w_m_down.astype(h.dtype)
    merged = jax.nn.sigmoid(ga) * ya + jax.nn.sigmoid(gb) * yb
    return merged @ w_o.astype(h.dtype), conv_buf_new, mconv_buf_new, C1, n1, m1


def swiglu(x, w1, w3, w2):
    return (jax.nn.silu(x @ w1) * (x @ w3)) @ w2


def moe(h, w_router, router_bias, w_e1, w_e3, w_e2, w_s1, w_s3, w_s2):
    T = h.shape[0]
    dt = h.dtype
    scores = jax.nn.sigmoid(h.astype(jnp.float32) @ w_router.astype(jnp.float32))
    choice = scores + router_bias.astype(jnp.float32)
    grp = choice.reshape(T, N_GROUPS, N_EXPERTS // N_GROUPS)
    grp_score = jnp.sum(lax.top_k(grp, 2)[0], axis=-1)
    _, top_g = lax.top_k(grp_score, TOPK_GROUPS)
    gmask = jnp.sum(jax.nn.one_hot(top_g, N_GROUPS, dtype=jnp.float32), axis=1) > 0
    gmask = jnp.repeat(gmask, N_EXPERTS // N_GROUPS, axis=1)
    choice = jnp.where(gmask, choice, -jnp.inf)
    _, top_e = lax.top_k(choice, TOP_K)
    wts = jnp.take_along_axis(scores, top_e, axis=1)
    wts = wts / jnp.sum(wts, axis=-1, keepdims=True) * ROUTE_SCALE
    A = T * TOP_K
    e_flat = top_e.reshape(-1).astype(jnp.int32)
    tok_flat = jnp.repeat(jnp.arange(T, dtype=jnp.int32), TOP_K)
    w_flat = wts.reshape(-1)
    order = jnp.argsort(e_flat)
    e_s, tok_s, w_s = e_flat[order], tok_flat[order], w_flat[order]
    counts = jnp.bincount(e_flat, length=N_EXPERTS)
    starts = jnp.cumsum(counts) - counts
    nblk_e = (counts + EXPERT_BLOCK - 1) // EXPERT_BLOCK
    blk_end = jnp.cumsum(nblk_e)
    pad_starts = (blk_end - nblk_e) * EXPERT_BLOCK
    pos = pad_starts[e_s] + (jnp.arange(A, dtype=jnp.int32) - starts[e_s])
    n_blocks = -(-A // EXPERT_BLOCK) + N_EXPERTS
    R = n_blocks * EXPERT_BLOCK
    row_tok = jnp.zeros((R,), jnp.int32).at[pos].set(tok_s)
    row_w = jnp.zeros((R,), jnp.float32).at[pos].set(w_s)
    blk_expert = jnp.minimum(jnp.searchsorted(blk_end, jnp.arange(n_blocks), side='right'), N_EXPERTS - 1)

    def expert_block(args):
        tok, w, e = args
        xb = h[tok]
        return swiglu(xb, w_e1[e].astype(dt), w_e3[e].astype(dt), w_e2[e].astype(dt)) * w[:, None].astype(dt)

    out = lax.map(expert_block, (row_tok.reshape(n_blocks, EXPERT_BLOCK), row_w.reshape(n_blocks, EXPERT_BLOCK), blk_expert))
    routed = jax.ops.segment_sum(out.reshape(R, -1), row_tok, num_segments=T)
    shared = swiglu(h, w_s1.astype(dt), w_s3.astype(dt), w_s2.astype(dt))
    return shared + routed


def setup_inputs(seed: int = 0) -> dict:
    key = jax.random.key(seed)
    ks = iter(jax.random.split(key, 48))
    f32 = jnp.float32
    nrm = lambda shape, scale: jax.random.normal(next(ks), shape, f32) * scale
    gain = lambda shape: 1.0 + nrm(shape, 0.02)
    L = DEPTH
    nb = D_INNER // QKV_BLOCK
    b_if = jnp.concatenate([nrm((L, N_HEADS), 0.1), 3.0 + 3.0 * jax.random.uniform(next(ks), (L, N_HEADS), f32)], axis=-1)
    return {
        'x_prompt': nrm((BATCH, SEQ, D_MODEL), 1.0),
        'x_sample': nrm((DEC_BATCH, DEC_SEQ, D_MODEL), 1.0),
        'state_conv': nrm((L, DEC_BATCH, CONV_K - 1, D_CONV), 0.5),
        'state_mconv': nrm((L, DEC_BATCH, MCONV_K - 1, D_INNER), 1.0),
        'state_C': nrm((L, DEC_BATCH, N_HEADS, HEAD_DIM, HEAD_DIM), 0.05),
        'state_n': nrm((L, DEC_BATCH, N_HEADS, HEAD_DIM), 0.1),
        'state_m': nrm((L, DEC_BATCH, N_HEADS), 1.0),
        'meta_tokens': nrm((N_META, D_MODEL), 1.0),
        'ln_in_g': gain((D_MODEL,)),
        'ln_in_b': nrm((D_MODEL,), 0.02),
        'w_in': nrm((L, D_MODEL, N_IN), D_MODEL ** -0.5),
        'conv_w': nrm((L, CONV_K, D_CONV), CONV_K ** -0.5),
        'conv_b': nrm((L, D_CONV), 0.02),
        'conv_ln_g': gain((L, D_CONV)),
        'conv_ln_b': nrm((L, D_CONV), 0.02),
        'w_conv_out': nrm((L, D_CONV, D_MODEL), BETA * D_CONV ** -0.5),
        'mconv_w': nrm((L, MCONV_K, D_INNER), MCONV_K ** -0.5),
        'mconv_b': nrm((L, D_INNER), 0.02),
        'w_q': nrm((L, nb, QKV_BLOCK, QKV_BLOCK), QKV_BLOCK ** -0.5),
        'w_k': nrm((L, nb, QKV_BLOCK, QKV_BLOCK), QKV_BLOCK ** -0.5),
        'w_v': nrm((L, nb, QKV_BLOCK, QKV_BLOCK), BETA * QKV_BLOCK ** -0.5),
        'w_if': nrm((L, 3 * D_INNER, 2 * N_HEADS), (3 * D_INNER) ** -0.5),
        'b_if': b_if,
        'mh_ln_g': gain((L, D_INNER)),
        'skip': gain((L, D_INNER)),
        'w_m_down': nrm((L, D_INNER, D_MODEL), BETA * D_INNER ** -0.5),
        'w_o': nrm((L, D_MODEL, D_MODEL), BETA * D_MODEL ** -0.5),
        'ln1_g': gain((L, D_MODEL)),
        'ln1_b': nrm((L, D_MODEL), 0.02),
        'ln2_g': gain((L, D_MODEL)),
        'ln2_b': nrm((L, D_MODEL), 0.02),
        'w_router': nrm((L, D_MODEL, N_EXPERTS), D_MODEL ** -0.5),
        'router_bias': nrm((L, N_EXPERTS), 0.01),
        'w_e1': nrm((L, N_EXPERTS, D_MODEL, D_EXPERT), D_MODEL ** -0.5),
        'w_e3': nrm((L, N_EXPERTS, D_MODEL, D_EXPERT), D_MODEL ** -0.5),
        'w_e2': nrm((L, N_EXPERTS, D_EXPERT, D_MODEL), BETA * D_EXPERT ** -0.5),
        'w_s1': nrm((L, D_MODEL, D_SHARED), D_MODEL ** -0.5),
        'w_s3': nrm((L, D_MODEL, D_SHARED), D_MODEL ** -0.5),
        'w_s2': nrm((L, D_SHARED, D_MODEL), BETA * D_SHARED ** -0.5),
    }


def reference(x_prompt, x_sample, state_conv, state_mconv, state_C, state_n, state_m,
              meta_tokens, ln_in_g, ln_in_b, w_in, conv_w, conv_b, conv_ln_g, conv_ln_b, w_conv_out,
              mconv_w, mconv_b, w_q, w_k, w_v, w_if, b_if, mh_ln_g, skip, w_m_down, w_o,
              ln1_g, ln1_b, ln2_g, ln2_b, w_router, router_bias, w_e1, w_e3, w_e2, w_s1, w_s3, w_s2):
    f32 = jnp.float32
    B = x_prompt.shape[0]
    DB = x_sample.shape[0]
    meta = jnp.broadcast_to(meta_tokens.astype(x_prompt.dtype)[None], (B, N_META, D_MODEL))
    hp = layer_norm(jnp.concatenate([meta, x_prompt], axis=1), ln_in_g, ln_in_b)
    hs = layer_norm(x_sample, ln_in_g, ln_in_b)
    Lp = hp.shape[1]
    Ls = hs.shape[1]
    zc = jnp.zeros((B, CONV_K - 1, D_CONV), hp.dtype)
    zmc = jnp.zeros((B, MCONV_K - 1, D_INNER), hp.dtype)
    zC = jnp.zeros((B, N_HEADS, HEAD_DIM, HEAD_DIM), f32)
    zn = jnp.zeros((B, N_HEADS, HEAD_DIM), f32)
    zm = jnp.zeros((B, N_HEADS), f32)
    conv_p, mconv_p, C_p, n_p, m_p = [], [], [], [], []
    conv_s, mconv_s, C_s, n_s, m_s = [], [], [], [], []
    for l in range(DEPTH):
        mw = (w_in[l], conv_w[l], conv_b[l], conv_ln_g[l], conv_ln_b[l], w_conv_out[l], mconv_w[l], mconv_b[l],
              w_q[l], w_k[l], w_v[l], w_if[l], b_if[l], mh_ln_g[l], skip[l], w_m_down[l], w_o[l])
        op, cb, mb, C1, n1, m1 = mix_sublayer(hp, zc, zmc, zC, zn, zm, N_META, CHUNK, *mw)
        conv_p.append(cb); mconv_p.append(mb); C_p.append(C1); n_p.append(n1); m_p.append(m1)
        os_, cb, mb, C1, n1, m1 = mix_sublayer(hs, state_conv[l], state_mconv[l], state_C[l], state_n[l], state_m[l], 0, Ls, *mw)
        conv_s.append(cb); mconv_s.append(mb); C_s.append(C1); n_s.append(n1); m_s.append(m1)
        hp = layer_norm(ALPHA * hp + op, ln1_g[l], ln1_b[l])
        hs = layer_norm(ALPHA * hs + os_, ln1_g[l], ln1_b[l])
        ew = (w_router[l], router_bias[l], w_e1[l], w_e3[l], w_e2[l], w_s1[l], w_s3[l], w_s2[l])
        fp = moe(hp.reshape(B * Lp, D_MODEL), *ew).reshape(B, Lp, D_MODEL)
        fs = moe(hs.reshape(DB * Ls, D_MODEL), *ew).reshape(DB, Ls, D_MODEL)
        hp = layer_norm(ALPHA * hp + fp, ln2_g[l], ln2_b[l])
        hs = layer_norm(ALPHA * hs + fs, ln2_g[l], ln2_b[l])
    y_prompt = hp[:, N_META:]
    y_sample = hs
    return (y_prompt, y_sample,
            jnp.stack(conv_p), jnp.stack(mconv_p), jnp.stack(C_p), jnp.stack(n_p), jnp.stack(m_p),
            jnp.stack(conv_s), jnp.stack(mconv_s), jnp.stack(C_s), jnp.stack(n_s), jnp.stack(m_s))
```

```python
import functools

import jax
import jax.numpy as jnp
from jax import lax
from jax.experimental import pallas as pl
from jax.experimental.pallas import tpu as pltpu

F32 = jnp.float32
BF16 = jnp.bfloat16
I32 = jnp.int32

LN_EPS = 1e-5
TOP_K = 8
N_GROUPS = 8
TOPK_GROUPS = 4
ROUTE_SCALE = 2.5

SUBLANES = 8
LANES = 128
MXU_DIM = 256
ROW_TILE = 512
SEQ_TILE = 256
EXPERT_ROWS = 256
VMEM_LIMIT = 48 * 1024 * 1024


def _round_up(x, m):
    return (x + m - 1) // m * m


def _params(sem):
    return pltpu.CompilerParams(dimension_semantics=sem, vmem_limit_bytes=VMEM_LIMIT)


def _ln(x, g, b):
    mu = jnp.mean(x, axis=-1, keepdims=True)
    xc = x - mu
    var = jnp.mean(xc * xc, axis=-1, keepdims=True)
    return xc * lax.rsqrt(var + LN_EPS) * g + b


def _silu(x):
    return x * jax.nn.sigmoid(x)


def _log_sigmoid(x):
    return jnp.minimum(x, 0.0) - jnp.log1p(jnp.exp(-jnp.abs(x)))


def _dot(a, b):
    return jnp.dot(a, b, preferred_element_type=F32)


def _dot_nt(a, b):
    return lax.dot_general(a, b, (((1,), (1,)), ((), ())), preferred_element_type=F32)


def _dot_tn(a, b):
    return lax.dot_general(a, b, (((0,), (0,)), ((), ())), preferred_element_type=F32)


def _full_spec(a):
    nd = a.ndim
    return pl.BlockSpec(a.shape, lambda *_: (0,) * nd)


def _rowwise(body, rows, ins, consts, outs, name):
    tm = ROW_TILE
    in_specs = [pl.BlockSpec((tm, w), functools.partial(lambda i, cb: (i, cb), cb=cb)) for _, w, cb in ins]
    in_specs += [_full_spec(c) for c in consts]
    out_specs = [pl.BlockSpec((tm, w), lambda i: (i, 0)) for w, _ in outs]
    out_shape = [jax.ShapeDtypeStruct((rows, w), dt) for w, dt in outs]
    return pl.pallas_call(
        body, out_shape=out_shape, grid=(rows // tm,), in_specs=in_specs, out_specs=out_specs,
        compiler_params=_params(("arbitrary",)), name=name,
    )(*[a for a, _, _ in ins], *consts)


def _ln_body(x_ref, g_ref, b_ref, o_ref):
    o_ref[...] = _ln(x_ref[...], g_ref[...], b_ref[...])


def _in_proj_body(x_ref, w_ref, o_ref):
    o_ref[...] = _dot(x_ref[...].astype(BF16), w_ref[...])


def _in_proj(x, w):
    rows, d = x.shape
    n = w.shape[1]
    tm, tn = ROW_TILE, 2048
    return pl.pallas_call(
        _in_proj_body, out_shape=jax.ShapeDtypeStruct((rows, n), F32), grid=(n // tn, rows // tm),
        in_specs=[pl.BlockSpec((tm, d), lambda j, i: (i, 0)), pl.BlockSpec((d, tn), lambda j, i: (0, j))],
        out_specs=pl.BlockSpec((tm, tn), lambda j, i: (i, j)),
        compiler_params=_params(("arbitrary", "arbitrary")), name="in_proj",
    )(x, w)


def _matmul_body(x_ref, w_ref, o_ref):
    o_ref[...] = _dot(x_ref[...].astype(BF16), w_ref[...])


def _qkv_body(cm_ref, xm_ref, wq_ref, wk_ref, wv_ref, wif_ref, wift_ref, bc_ref, br_ref,
              q_ref, k_ref, v_ref, gc_ref, gr_ref):
    di = cm_ref.shape[1]
    nb = di // MXU_DIM
    gc = jnp.zeros(gc_ref.shape, F32) + bc_ref[...]
    gr = jnp.zeros(gr_ref.shape, F32) + br_ref[...]
    for j in range(nb):
        sl = slice(j * MXU_DIM, (j + 1) * MXU_DIM)
        cmj = cm_ref[:, sl].astype(BF16)
        xmj = xm_ref[:, sl].astype(BF16)
        for o_ref, src, w_ref, seg in ((q_ref, cmj, wq_ref, 0), (k_ref, cmj, wk_ref, 1), (v_ref, xmj, wv_ref, 2)):
            y = _dot(src, w_ref[j])
            o_ref[:, sl] = y
            yb = y.astype(BF16)
            ws = slice(seg * di + j * MXU_DIM, seg * di + (j + 1) * MXU_DIM)
            gc = gc + _dot(yb, wif_ref[ws, :])
            gr = gr + _dot_nt(wift_ref[:, ws], yb)
    gc_ref[...] = gc
    gr_ref[...] = gr


def _qkv(cm, proj, xm_block, wq, wk, wv, wif, wift, bc, br):
    rows, di = cm.shape
    tm = ROW_TILE
    row_spec = lambda w, cb: pl.BlockSpec((tm, w), functools.partial(lambda i, cb: (i, cb), cb=cb))
    consts = (wq, wk, wv, wif, wift, bc, br)
    return pl.pallas_call(
        _qkv_body,
        out_shape=[jax.ShapeDtypeStruct((rows, di), F32)] * 3
        + [jax.ShapeDtypeStruct((rows, LANES), F32), jax.ShapeDtypeStruct((SUBLANES, rows), F32)],
        grid=(rows // tm,),
        in_specs=[row_spec(di, 0), row_spec(di, xm_block)] + [_full_spec(c) for c in consts],
        out_specs=[row_spec(di, 0)] * 3 + [row_spec(LANES, 0), pl.BlockSpec((SUBLANES, tm), lambda i: (0, i))],
        compiler_params=_params(("arbitrary",)), name="qkv_gates",
    )(cm, proj, *consts)


def _down_body(hh_ref, cm_ref, og_ref, g_ref, skip_ref, w_ref, o_ref):
    hb = (hh_ref[...] * g_ref[...] + skip_ref[...] * cm_ref[...]) * jax.nn.sigmoid(og_ref[...])
    o_ref[...] = _dot(hb.astype(BF16), w_ref[...])


def _merge_body(alpha, ya_ref, yb_ref, ga_ref, gb_ref, x_ref, w_ref, g_ref, b_ref, o_ref):
    merged = jax.nn.sigmoid(ga_ref[...]) * ya_ref[...] + jax.nn.sigmoid(gb_ref[...]) * yb_ref[...]
    o = _dot(merged.astype(BF16), w_ref[...])
    o_ref[...] = _ln(alpha * x_ref[...] + o, g_ref[...], b_ref[...])


def _glu_pre(a, b):
    return a * jax.nn.sigmoid(b)


def _conv_taps(k_taps):
    hp = _round_up(k_taps - 1, SUBLANES)
    return hp, hp - (k_taps - 1)


def _conv_window(s_ref, w_ref, bias, n, k_taps):
    _, off = _conv_taps(k_taps)
    acc = jnp.zeros((n, s_ref.shape[1]), F32) + bias
    for k in range(k_taps):
        acc = acc + w_ref[k:k + 1, :] * s_ref[k + off:k + off + n, :]
    return acc


def _conv_long(s_ref, sh_ref, w_ref, bias, out_ref, post, n, k_taps):
    hp, off = _conv_taps(k_taps)
    c = s_ref.shape[1]
    shifts = sorted({(k + off) % SUBLANES for k in range(k_taps)} - {0})
    span = n + hp - SUBLANES
    for s in shifts:
        sh_ref[s, 0:span, :] = s_ref[s:s + span, :]
    rg = SUBLANES * max(1, min(4, 32 * 1024 // (SUBLANES * c)))
    while n % rg:
        rg //= 2

    def body(i, carry):
        t0 = pl.multiple_of(i * rg, rg)
        acc = jnp.zeros((rg, c), F32) + bias
        for k in range(k_taps):
            g, s = divmod(k + off, SUBLANES)
            if s == 0:
                x = s_ref[pl.ds(t0 + SUBLANES * g, rg), :]
            else:
                x = sh_ref[s, pl.ds(t0 + SUBLANES * g, rg), :]
            acc = acc + w_ref[k:k + 1, :] * x
        out_ref[pl.ds(t0, rg), :] = post(acc)
        return carry

    lax.fori_loop(0, n // rg, body, 0)


def _conv_prompt_body(n_pre, k_taps, post, *refs):
    main = refs[:n_pre]
    meta = refs[n_pre:2 * n_pre]
    w_ref, b_ref = refs[2 * n_pre:2 * n_pre + 2]
    extra = refs[2 * n_pre + 2:-5]
    out_main, out_meta, state_ref, s_ref, sh_ref = refs[-5:]
    pre = _glu_pre if n_pre == 2 else (lambda a: a)
    post_fn = functools.partial(post, *[e[...] for e in extra]) if extra else post
    hp, off = _conv_taps(k_taps)
    n_meta, tl = out_meta.shape[0], out_main.shape[0]
    c = pl.program_id(1)
    bias = b_ref[...]

    @pl.when(c == 0)
    def _():
        s_ref[0:hp, :] = jnp.zeros((hp, s_ref.shape[1]), F32)
        s_ref[hp:hp + n_meta, :] = pre(*[r[...] for r in meta])
        out_meta[...] = post_fn(_conv_window(s_ref, w_ref, bias, n_meta, k_taps))
        s_ref[0:hp, :] = s_ref[n_meta:n_meta + hp, :]

    @pl.when(c > 0)
    def _():
        s_ref[hp:hp + tl, :] = pre(*[r[...] for r in main])
        _conv_long(s_ref, sh_ref, w_ref, bias, out_main, post_fn, tl, k_taps)
        s_ref[0:hp, :] = s_ref[tl:tl + hp, :]

    @pl.when(c == pl.num_programs(1) - 1)
    def _():
        state_ref[0] = s_ref[off:hp, :]


def _conv_prompt(proj, col_blocks, w, b, extra, post, dims, name):
    bsz, seq, n_meta = dims
    k_taps, c = w.shape
    hp, _ = _conv_taps(k_taps)
    tl = min(SEQ_TILE, seq)
    nc = seq // tl
    meta_blk0 = bsz * seq // n_meta
    n_pre = len(col_blocks)
    main_map = lambda cb: (lambda bi, ci: (bi * nc + jnp.maximum(ci - 1, 0), cb))
    meta_map = lambda cb: (lambda bi, ci: (meta_blk0 + bi, cb))
    in_specs = [pl.BlockSpec((tl, c), main_map(cb)) for cb in col_blocks]
    in_specs += [pl.BlockSpec((n_meta, c), meta_map(cb)) for cb in col_blocks]
    consts = (w, b) + tuple(extra)
    in_specs += [_full_spec(a) for a in consts]
    return pl.pallas_call(
        functools.partial(_conv_prompt_body, n_pre, k_taps, post),
        out_shape=[jax.ShapeDtypeStruct((bsz * seq, c), F32), jax.ShapeDtypeStruct((bsz * n_meta, c), F32),
                   jax.ShapeDtypeStruct((bsz, k_taps - 1, c), F32)],
        grid=(bsz, 1 + nc), in_specs=in_specs,
        out_specs=[pl.BlockSpec((tl, c), main_map(0)), pl.BlockSpec((n_meta, c), lambda bi, ci: (bi, 0)),
                   pl.BlockSpec((1, k_taps - 1, c), lambda bi, ci: (bi, 0, 0))],
        scratch_shapes=[pltpu.VMEM((hp + tl, c), F32), pltpu.VMEM((SUBLANES, hp + tl, c), F32)],
        compiler_params=_params(("arbitrary", "arbitrary")), name=name,
    )(*([proj] * (2 * n_pre)), *consts)


def _conv_sample_body(n_pre, k_taps, post, seqs, *refs):
    rows = refs[:n_pre]
    hist_ref, w_ref, b_ref = refs[n_pre:n_pre + 3]
    extra = refs[n_pre + 3:-3]
    out_ref, state_ref, s_ref = refs[-3:]
    pre = _glu_pre if n_pre == 2 else (lambda a: a)
    post_fn = functools.partial(post, *[e[...] for e in extra]) if extra else post
    hp, off = _conv_taps(k_taps)
    n = out_ref.shape[0] // seqs
    bias = b_ref[...]

    def body(j, carry):
        r0 = pl.multiple_of(j * n, n)
        s_ref[off:hp, :] = hist_ref[j]
        s_ref[hp:hp + n, :] = pre(*[r[pl.ds(r0, n), :] for r in rows])
        out_ref[pl.ds(r0, n), :] = post_fn(_conv_window(s_ref, w_ref, bias, n, k_taps))
        state_ref[j] = s_ref[off + n:hp + n, :]
        return carry

    lax.fori_loop(0, seqs, body, 0)


def _conv_sample(proj, col_blocks, hist, w, b, extra, post, dims, name):
    row0, dbsz, n = dims
    k_taps, c = w.shape
    hp, _ = _conv_taps(k_taps)
    seqs = min(SUBLANES, dbsz)
    blk0 = row0 // (seqs * n)
    n_pre = len(col_blocks)
    consts = (w, b) + tuple(extra)
    in_specs = [pl.BlockSpec((seqs * n, c), functools.partial(lambda i, cb: (blk0 + i, cb), cb=cb)) for cb in col_blocks]
    in_specs += [pl.BlockSpec((seqs, k_taps - 1, c), lambda i: (i, 0, 0))]
    in_specs += [_full_spec(a) for a in consts]
    return pl.pallas_call(
        functools.partial(_conv_sample_body, n_pre, k_taps, post, seqs),
        out_shape=[jax.ShapeDtypeStruct((dbsz * n, c), F32), jax.ShapeDtypeStruct((dbsz, k_taps - 1, c), F32)],
        grid=(dbsz // seqs,), in_specs=in_specs,
        out_specs=[pl.BlockSpec((seqs * n, c), lambda i: (i, 0)), pl.BlockSpec((seqs, k_taps - 1, c), lambda i: (i, 0, 0))],
        scratch_shapes=[pltpu.VMEM((hp + n, c), F32)],
        compiler_params=_params(("arbitrary",)), name=name,
    )(*([proj] * n_pre), hist, *consts)


def _post_ln_silu(g, b, acc):
    return _silu(_ln(acc, g, b))


def _mlstm_chunk(gc, gr, q, k, v, c0, n0, m0, hsel, n_heads):
    seq, hd = q.shape
    lane = lax.broadcasted_iota(I32, gc.shape, 1)
    sub = lax.broadcasted_iota(I32, gr.shape, 0)
    pick_c = lambda j: jnp.sum(jnp.where(lane == j, gc, 0.0), axis=1, keepdims=True)
    pick_r = lambda j: jnp.sum(jnp.where(sub == j, gr, 0.0), axis=0, keepdims=True)
    ig_c, lf_c = pick_c(hsel), _log_sigmoid(pick_c(hsel + n_heads))
    ig_r, lf_r = pick_r(hsel), _log_sigmoid(pick_r(hsel + n_heads))
    ti = lax.broadcasted_iota(I32, (seq, seq), 0)
    si = lax.broadcasted_iota(I32, (seq, seq), 1)
    causal = si <= ti
    b_c = jnp.sum(jnp.where(causal, lf_r, 0.0), axis=1, keepdims=True)
    b_r = jnp.sum(jnp.where(ti <= si, lf_c, 0.0), axis=0, keepdims=True)
    log_w = jnp.where(causal, b_c - b_r + ig_r, -jnp.inf)
    log_s = b_c + m0
    m_c = jnp.maximum(log_s, jnp.max(log_w, axis=1, keepdims=True))
    w = jnp.exp(log_w - m_c)
    s_c = jnp.exp(log_s - m_c)
    ks = k * (hd ** -0.5)
    qb, kb, vb = q.astype(BF16), ks.astype(BF16), v.astype(BF16)
    qk = _dot_nt(qb, kb) * w
    num = s_c * _dot(qb, c0.astype(BF16)) + _dot(qk.astype(BF16), vb)
    den = s_c * jnp.sum(q * n0, axis=1, keepdims=True) + jnp.sum(qk, axis=1, keepdims=True)
    hval = num / jnp.maximum(jnp.abs(den), jnp.exp(-m_c))
    mu = jnp.mean(hval, axis=1, keepdims=True)
    hc = hval - mu
    var = jnp.mean(hc * hc, axis=1, keepdims=True)
    hh = hc * lax.rsqrt(var + LN_EPS)
    m_end = m_c[seq - 1:seq, :]
    b_last = b_c[seq - 1:seq, :]
    w_end = jnp.exp(b_last - b_c + ig_c - m_end)
    s_end = jnp.exp(b_last + m0 - m_end)
    kw = ks * w_end
    c1 = s_end * c0 + _dot_tn(kw.astype(BF16), vb)
    n1 = s_end * n0 + jnp.sum(kw, axis=0, keepdims=True)
    return hh, c1, n1, m_end


def _mlstm_prompt_body(n_heads, gcm_ref, grm_ref, qm_ref, km_ref, vm_ref, gc_ref, gr_ref, q_ref, k_ref, v_ref,
                       hm_ref, h_ref, c_out, n_out, m_out, c_s, n_s, m_s):
    hsel = pl.program_id(1)
    c = pl.program_id(2)

    def step(gc, gr, q, k, v, out):
        hh, c1, n1, m1 = _mlstm_chunk(gc, gr, q, k, v, c_s[...], n_s[...], m_s[0:1, 0:1], hsel, n_heads)
        out[...] = hh
        c_s[...] = c1
        n_s[...] = n1
        m_s[...] = jnp.broadcast_to(m1, m_s.shape)

    @pl.when(c == 0)
    def _():
        c_s[...] = jnp.zeros(c_s.shape, F32)
        n_s[...] = jnp.zeros(n_s.shape, F32)
        m_s[...] = jnp.zeros(m_s.shape, F32)
        step(gcm_ref[...], grm_ref[0], qm_ref[...], km_ref[...], vm_ref[...], hm_ref)

    @pl.when(c > 0)
    def _():
        step(gc_ref[...], gr_ref[...], q_ref[...], k_ref[...], v_ref[...], h_ref)

    @pl.when(c == pl.num_programs(2) - 1)
    def _():
        c_out[0] = c_s[...]
        n_out[0] = n_s[...]
        m_out[0] = m_s[...]


def _mlstm_prompt(q, k, v, gcol, grow, grow_meta, dims, n_heads):
    bsz, seq, n_meta = dims
    hd = q.shape[1] // n_heads
    tl = min(SEQ_TILE, seq)
    nc = seq // tl
    meta_blk0 = bsz * seq // n_meta
    main_r = lambda bi, hi, ci: bi * nc + jnp.maximum(ci - 1, 0)
    qkv_main = pl.BlockSpec((tl, hd), lambda bi, hi, ci: (main_r(bi, hi, ci), hi))
    qkv_meta = pl.BlockSpec((n_meta, hd), lambda bi, hi, ci: (meta_blk0 + bi, hi))
    st = lambda shape: pl.BlockSpec((1,) + shape, lambda bi, hi, ci: (bi * n_heads + hi, 0, 0))
    return pl.pallas_call(
        functools.partial(_mlstm_prompt_body, n_heads),
        out_shape=[jax.ShapeDtypeStruct((bsz * n_meta, q.shape[1]), F32), jax.ShapeDtypeStruct((bsz * seq, q.shape[1]), F32),
                   jax.ShapeDtypeStruct((bsz * n_heads, hd, hd), F32), jax.ShapeDtypeStruct((bsz * n_heads, 1, hd), F32),
                   jax.ShapeDtypeStruct((bsz * n_heads, 1, LANES), F32)],
        grid=(bsz, n_heads, 1 + nc),
        in_specs=[pl.BlockSpec((n_meta, LANES), lambda bi, hi, ci: (meta_blk0 + bi, 0)),
                  pl.BlockSpec((1, SUBLANES, n_meta), lambda bi, hi, ci: (bi, 0, 0)),
                  qkv_meta, qkv_meta, qkv_meta,
                  pl.BlockSpec((tl, LANES), lambda bi, hi, ci: (main_r(bi, hi, ci), 0)),
                  pl.BlockSpec((SUBLANES, tl), lambda bi, hi, ci: (0, main_r(bi, hi, ci))),
                  qkv_main, qkv_main, qkv_main],
        out_specs=[pl.BlockSpec((n_meta, hd), lambda bi, hi, ci: (bi, hi)), qkv_main,
                   st((hd, hd)), st((1, hd)), st((1, LANES))],
        scratch_shapes=[pltpu.VMEM((hd, hd), F32), pltpu.VMEM((1, hd), F32), pltpu.VMEM((1, LANES), F32)],
        compiler_params=_params(("arbitrary",) * 3), name="mlstm_prompt",
    )(gcol, grow_meta, q, k, v, gcol, grow, q, k, v)


def _mlstm_sample_body(n_heads, gc_ref, gr_ref, q_ref, k_ref, v_ref, c0_ref, n0_ref, m0_ref,
                       h_ref, c_out, n_out, m_out):
    hh, c1, n1, m1 = _mlstm_chunk(gc_ref[...], gr_ref[0], q_ref[...], k_ref[...], v_ref[...],
                                  c0_ref[0], n0_ref[0], m0_ref[0][0:1, 0:1], pl.program_id(1), n_heads)
    h_ref[...] = hh
    c_out[0] = c1
    n_out[0] = n1
    m_out[0] = jnp.broadcast_to(m1, m_out.shape[1:])


def _mlstm_sample(q, k, v, gcol, grow_s, c0, n0, m0, dims, n_heads):
    row0, dbsz, n = dims
    hd = q.shape[1] // n_heads
    blk0 = row0 // n
    qkv = pl.BlockSpec((n, hd), lambda bi, hi: (blk0 + bi, hi))
    st = lambda shape: pl.BlockSpec((1,) + shape, lambda bi, hi: (bi * n_heads + hi, 0, 0))
    return pl.pallas_call(
        functools.partial(_mlstm_sample_body, n_heads),
        out_shape=[jax.ShapeDtypeStruct((dbsz * n, q.shape[1]), F32),
                   jax.ShapeDtypeStruct((dbsz * n_heads, hd, hd), F32), jax.ShapeDtypeStruct((dbsz * n_heads, 1, hd), F32),
                   jax.ShapeDtypeStruct((dbsz * n_heads, 1, LANES), F32)],
        grid=(dbsz, n_heads),
        in_specs=[pl.BlockSpec((n, LANES), lambda bi, hi: (blk0 + bi, 0)),
                  pl.BlockSpec((1, SUBLANES, n), lambda bi, hi: (bi, 0, 0)),
                  qkv, qkv, qkv, st((hd, hd)), st((1, hd)), st((1, LANES))],
        out_specs=[pl.BlockSpec((n, hd), lambda bi, hi: (bi, hi)), st((hd, hd)), st((1, hd)), st((1, LANES))],
        compiler_params=_params(("arbitrary",) * 2), name="mlstm_sample",
    )(gcol, grow_s, q, k, v, c0, n0, m0)


def _stack_rows(rows, n):
    sub = lax.broadcasted_iota(I32, (len(rows), n), 0)
    out = jnp.zeros((len(rows), n), rows[0].dtype)
    for j, r in enumerate(rows):
        out = jnp.where(sub == j, r, out)
    return out


def _router_body(h_ref, wt_ref, bias_ref, te_ref, wk_ref):
    n_exp = wt_ref.shape[0]
    gsz = n_exp // N_GROUPS
    tm = h_ref.shape[0]
    logits = _dot_nt(wt_ref[...], h_ref[...].astype(BF16))
    scores = jax.nn.sigmoid(logits)
    choice = scores + bias_ref[...]
    sub = lax.broadcasted_iota(I32, (gsz, tm), 0)
    neg = -jnp.inf

    def first_max(x):
        m = jnp.max(x, axis=0, keepdims=True)
        idx = jnp.min(jnp.where(x == m, sub, gsz), axis=0, keepdims=True)
        return m, idx

    sc = [scores[g * gsz:(g + 1) * gsz, :] for g in range(N_GROUPS)]
    ch = [choice[g * gsz:(g + 1) * gsz, :] for g in range(N_GROUPS)]
    gs = []
    for x in ch:
        m1, i1 = first_max(x)
        m2, _ = first_max(jnp.where(sub == i1, neg, x))
        gs.append(m1 + m2)
    y = _stack_rows(gs, tm)
    gsub = lax.broadcasted_iota(I32, (N_GROUPS, tm), 0)
    gsel = jnp.zeros((N_GROUPS, tm), jnp.bool_)
    for _ in range(TOPK_GROUPS):
        m = jnp.max(y, axis=0, keepdims=True)
        idx = jnp.min(jnp.where(y == m, gsub, N_GROUPS), axis=0, keepdims=True)
        hit = gsub == idx
        gsel = jnp.logical_or(gsel, hit)
        y = jnp.where(hit, neg, y)
    gself = gsel.astype(F32)
    ch = [jnp.where(gself[g:g + 1, :] > 0.0, ch[g], neg) for g in range(N_GROUPS)]
    eid = [sub + g * gsz for g in range(N_GROUPS)]
    picks, pick_w = [], []
    for _ in range(TOP_K):
        m = functools.reduce(jnp.maximum, [jnp.max(x, axis=0, keepdims=True) for x in ch])
        idx = functools.reduce(jnp.minimum, [jnp.min(jnp.where(x == m, e, n_exp), axis=0, keepdims=True)
                                             for x, e in zip(ch, eid)])
        hits = [e == idx for e in eid]
        pick_w.append(functools.reduce(jnp.add, [jnp.sum(jnp.where(h, s, 0.0), axis=0, keepdims=True)
                                                 for h, s in zip(hits, sc)]))
        ch = [jnp.where(h, neg, x) for h, x in zip(hits, ch)]
        picks.append(idx)
    total = functools.reduce(jnp.add, pick_w)
    te_ref[...] = _stack_rows(picks, tm)
    wk_ref[...] = _stack_rows([w / total * ROUTE_SCALE for w in pick_w], tm)


def _router(h, wt, bias):
    rows, d = h.shape
    tm = ROW_TILE
    return pl.pallas_call(
        _router_body,
        out_shape=[jax.ShapeDtypeStruct((TOP_K, rows), I32), jax.ShapeDtypeStruct((TOP_K, rows), F32)],
        grid=(rows // tm,),
        in_specs=[pl.BlockSpec((tm, d), lambda i: (i, 0)), _full_spec(wt), _full_spec(bias)],
        out_specs=[pl.BlockSpec((TOP_K, tm), lambda i: (0, i))] * 2,
        compiler_params=_params(("arbitrary",)), name="router",
    )(h, wt, bias)


def _dispatch_body(n_blocks, te_ref, dest_ref, blke_ref, used_ref, start_ref, cnt_ref, cnt_s, start_s, carry_s):
    n_exp = cnt_s.shape[0]
    tw = te_ref.shape[1]
    phase, i = pl.program_id(0), pl.program_id(1)
    te = te_ref[...]
    eio = lax.broadcasted_iota(I32, (n_exp, tw), 0)
    sel = functools.reduce(jnp.logical_or, [eio == te[k:k + 1, :] for k in range(TOP_K)])
    self = sel.astype(F32)

    @pl.when(jnp.logical_and(phase == 0, i == 0))
    def _():
        cnt_s[...] = jnp.zeros(cnt_s.shape, F32)

    @pl.when(phase == 0)
    def _():
        cnt_s[...] += jnp.sum(self, axis=1, keepdims=True)

    @pl.when(jnp.logical_and(phase == 1, i == 0))
    def _():
        cnt = cnt_s[...]
        nblk = jnp.floor((cnt + (EXPERT_ROWS - 1)) * (1.0 / EXPERT_ROWS))
        r = lax.broadcasted_iota(I32, (n_exp, n_exp), 0)
        c = lax.broadcasted_iota(I32, (n_exp, n_exp), 1)
        lower = (c < r).astype(BF16)
        blk_start = _dot(lower, nblk.astype(BF16))
        blk_end = blk_start + nblk
        start_s[...] = blk_start * EXPERT_ROWS
        carry_s[...] = jnp.zeros(carry_s.shape, F32)
        start_ref[...] = (blk_start * EXPERT_ROWS).astype(I32)
        cnt_ref[...] = cnt.astype(I32)
        used_ref[...] = jnp.sum(nblk, axis=0, keepdims=True).astype(I32)
        jio = lax.broadcasted_iota(I32, (n_exp, blke_ref.shape[1]), 1).astype(F32)
        be = jnp.sum((blk_end[:, 0:1] <= jio).astype(F32), axis=0, keepdims=True)
        blke_ref[...] = jnp.minimum(be, n_exp - 1.0).astype(I32)

    @pl.when(phase == 1)
    def _():
        r = lax.broadcasted_iota(I32, (tw, tw), 0)
        c = lax.broadcasted_iota(I32, (tw, tw), 1)
        upper = (r <= c).astype(BF16)
        incl = _dot(self.astype(BF16), upper)
        dest = start_s[:, 0:1] + carry_s[:, 0:1] + incl - self
        carry_s[...] += jnp.sum(self, axis=1, keepdims=True)
        rows = [jnp.sum(jnp.where(eio == te[k:k + 1, :], dest, 0.0), axis=0, keepdims=True) for k in range(TOP_K)]
        dest_ref[...] = _stack_rows(rows, tw).astype(I32)


def _dispatch(te, n_exp, n_blocks):
    rows = te.shape[1]
    tw = ROW_TILE
    nbp = _round_up(n_blocks, LANES)
    small = lambda shape: pl.BlockSpec(shape, lambda p, i: (0, 0))
    return pl.pallas_call(
        functools.partial(_dispatch_body, n_blocks),
        out_shape=[jax.ShapeDtypeStruct((TOP_K, rows), I32), jax.ShapeDtypeStruct((1, nbp), I32),
                   jax.ShapeDtypeStruct((1, LANES), I32), jax.ShapeDtypeStruct((n_exp, LANES), I32),
                   jax.ShapeDtypeStruct((n_exp, LANES), I32)],
        grid=(2, rows // tw),
        in_specs=[pl.BlockSpec((TOP_K, tw), lambda p, i: (0, i))],
        out_specs=[pl.BlockSpec((TOP_K, tw), lambda p, i: (0, i * p)), small((1, nbp)), small((1, LANES)),
                   small((n_exp, LANES)), small((n_exp, LANES))],
        scratch_shapes=[pltpu.VMEM((n_exp, LANES), F32)] * 3,
        compiler_params=_params(("arbitrary", "arbitrary")), name="dispatch",
    )(te)


def _scatter_body(start_ref, cnt_ref, dest_ref, h_ref, xs_ref, zero_ref, sem):
    tm = h_ref.shape[0]
    row_copy = lambda t, r: pltpu.make_async_copy(h_ref.at[pl.ds(t, 1)], xs_ref.at[pl.ds(r, 1)], sem)

    def issue(t, carry):
        for k in range(TOP_K):
            row_copy(t, dest_ref[k, t]).start()
        return carry

    def drain(t, carry):
        for k in range(TOP_K):
            row_copy(0, 0).wait()
        return carry

    lax.fori_loop(0, tm, issue, 0)
    lax.fori_loop(0, tm, drain, 0)

    @pl.when(pl.program_id(0) == pl.num_programs(0) - 1)
    def _():
        zero_ref[...] = jnp.zeros(zero_ref.shape, F32)
        sizes = [SUBLANES << b for b in range((EXPERT_ROWS // SUBLANES).bit_length() - 1)]

        def fill(e, carry):
            cnt = cnt_ref[e]
            pad = (-cnt) & (EXPERT_ROWS - 1)
            row = start_ref[e] + cnt
            head = jnp.minimum((-row) & (SUBLANES - 1), pad)
            for r in range(SUBLANES - 1):
                cp = pltpu.make_async_copy(zero_ref.at[pl.ds(0, 1)], xs_ref.at[pl.ds(row + r, 1)], sem)

                @pl.when(r < head)
                def _():
                    cp.start()
                    cp.wait()

            row = row + head
            rest = pad - head
            for sz in sizes:
                cp = pltpu.make_async_copy(zero_ref.at[pl.ds(0, sz)],
                                           xs_ref.at[pl.ds(pl.multiple_of(row, SUBLANES), sz)], sem)

                @pl.when((rest & sz) != 0)
                def _():
                    cp.start()
                    cp.wait()

                row = row + (rest & sz)
            return carry

        lax.fori_loop(0, start_ref.shape[0], fill, 0)


def _scatter(h, dest, start, cnt, n_blocks):
    rows, d = h.shape
    tm = ROW_TILE // 2
    grid_spec = pltpu.PrefetchScalarGridSpec(
        num_scalar_prefetch=2, grid=(rows // tm,),
        in_specs=[pl.BlockSpec((TOP_K, tm), lambda i, *_: (0, i), memory_space=pltpu.SMEM),
                  pl.BlockSpec((tm, d), lambda i, *_: (i, 0))],
        out_specs=pl.BlockSpec(memory_space=pl.ANY),
        scratch_shapes=[pltpu.VMEM((EXPERT_ROWS // 2, d), F32), pltpu.SemaphoreType.DMA],
    )
    return pl.pallas_call(
        _scatter_body, out_shape=jax.ShapeDtypeStruct((n_blocks * EXPERT_ROWS, d), F32), grid_spec=grid_spec,
        compiler_params=_params(("arbitrary",)), name="moe_scatter",
    )(start, cnt, dest, h)


def _expert_body(blke_ref, used_ref, xs_ref, w1_ref, w3_ref, w2_ref, ys_ref, w1_s, w3_s, w2_s):
    j = pl.program_id(0)
    prev = blke_ref[jnp.maximum(j - 1, 0)]
    live = j < used_ref[0]

    @pl.when(jnp.logical_and(live, jnp.logical_or(j == 0, blke_ref[j] != prev)))
    def _():
        w1_s[...] = w1_ref[0].astype(BF16)
        w3_s[...] = w3_ref[0].astype(BF16)
        w2_s[...] = w2_ref[0].astype(BF16)

    @pl.when(live)
    def _():
        x = xs_ref[...].astype(BF16)
        mid = _silu(_dot(x, w1_s[...])) * _dot(x, w3_s[...])
        ys_ref[...] = _dot(mid.astype(BF16), w2_s[...])


def _experts(xs, blke, used, w1, w3, w2):
    rows, d = xs.shape
    de = w1.shape[2]
    n_blocks = rows // EXPERT_ROWS
    blk = lambda j, blke_ref, used_ref: jnp.minimum(j, used_ref[0] - 1)
    xmap = lambda j, b, u: (blk(j, b, u), 0)
    wmap = lambda j, b, u: (b[blk(j, b, u)], 0, 0)
    grid_spec = pltpu.PrefetchScalarGridSpec(
        num_scalar_prefetch=2, grid=(n_blocks,),
        in_specs=[pl.BlockSpec((EXPERT_ROWS, d), xmap), pl.BlockSpec((1, d, de), wmap),
                  pl.BlockSpec((1, d, de), wmap), pl.BlockSpec((1, de, d), wmap)],
        out_specs=pl.BlockSpec((EXPERT_ROWS, d), xmap),
        scratch_shapes=[pltpu.VMEM((d, de), BF16), pltpu.VMEM((d, de), BF16), pltpu.VMEM((de, d), BF16)],
    )
    return pl.pallas_call(
        _expert_body, out_shape=jax.ShapeDtypeStruct((rows, d), F32), grid_spec=grid_spec,
        compiler_params=_params(("arbitrary",)), name="moe_experts",
    )(blke, used, xs, w1, w3, w2)


def _combine_body(alpha, dest_ref, wk_ref, h_ref, ys_ref, ws1_ref, ws3_ref, ws2_ref, g_ref, b_ref, o_ref, buf, sem):
    tm = h_ref.shape[0]
    row_copy = lambda k, t, r: pltpu.make_async_copy(ys_ref.at[pl.ds(r, 1)], buf.at[k, pl.ds(t, 1)], sem)

    def issue(t, carry):
        for k in range(TOP_K):
            row_copy(k, t, dest_ref[k, t]).start()
        return carry

    def drain(t, carry):
        for k in range(TOP_K):
            row_copy(k, 0, 0).wait()
        return carry

    lax.fori_loop(0, tm, issue, 0)
    h = h_ref[...]
    hb = h.astype(BF16)
    f = _dot((_silu(_dot(hb, ws1_ref[...])) * _dot(hb, ws3_ref[...])).astype(BF16), ws2_ref[...])
    lax.fori_loop(0, tm, drain, 0)
    for k in range(TOP_K):
        f = f + wk_ref[:, k:k + 1] * buf[k]
    o_ref[...] = _ln(alpha * h + f, g_ref[...], b_ref[...])


def _combine(alpha, h, ys, dest, wk_col, ws1, ws3, ws2, g, b):
    rows, d = h.shape
    tm = ROW_TILE // 4
    consts = (ws1, ws3, ws2, g, b)
    return pl.pallas_call(
        functools.partial(_combine_body, alpha), out_shape=jax.ShapeDtypeStruct((rows, d), F32), grid=(rows // tm,),
        in_specs=[pl.BlockSpec((TOP_K, tm), lambda i: (0, i), memory_space=pltpu.SMEM),
                  pl.BlockSpec((tm, TOP_K), lambda i: (i, 0)), pl.BlockSpec((tm, d), lambda i: (i, 0)),
                  pl.BlockSpec(memory_space=pl.ANY)] + [_full_spec(c) for c in consts],
        out_specs=pl.BlockSpec((tm, d), lambda i: (i, 0)),
        scratch_shapes=[pltpu.VMEM((TOP_K, tm, d), F32), pltpu.SemaphoreType.DMA],
        compiler_params=_params(("arbitrary",)), name="moe_combine",
    )(dest, wk_col, h, ys, *consts)


def _moe(alpha, h, wrt, rbias, w1, w3, w2, ws1, ws3, ws2, g, b):
    rows = h.shape[0]
    n_exp = wrt.shape[0]
    n_blocks = -(-rows * TOP_K // EXPERT_ROWS) + n_exp
    te, wk = _router(h, wrt, rbias)
    dest, blke, used, start, cnt = _dispatch(te, n_exp, n_blocks)
    xs = _scatter(h, dest, start[:, 0], cnt[:, 0], n_blocks)
    ys = _experts(xs, blke[0, :n_blocks], used[0, :1], w1, w3, w2)
    return _combine(alpha, h, ys, dest, wk.T, ws1, ws3, ws2, g, b)


def _block_diag(w):
    nb, qb, _ = w.shape
    per = MXU_DIM // qb
    wt = w.reshape(nb // per, per, qb, qb)
    eye = jnp.eye(per, dtype=w.dtype)
    full = wt[:, :, :, None, :] * eye[None, :, None, :, None]
    return full.reshape(nb // per, MXU_DIM, MXU_DIM).astype(BF16)


def kernel(x_prompt, x_sample, state_conv, state_mconv, state_C, state_n, state_m, meta_tokens, ln_in_g, ln_in_b, w_in, conv_w, conv_b, conv_ln_g, conv_ln_b, w_conv_out, mconv_w, mconv_b, w_q, w_k, w_v, w_if, b_if, mh_ln_g, skip, w_m_down, w_o, ln1_g, ln1_b, ln2_g, ln2_b, w_router, router_bias, w_e1, w_e3, w_e2, w_s1, w_s3, w_s2):
    bsz, seq, d = x_prompt.shape
    dbsz, dseq, _ = x_sample.shape
    depth = w_in.shape[0]
    n_meta = meta_tokens.shape[0]
    dc = conv_w.shape[2]
    di = mconv_w.shape[2]
    n_heads = state_m.shape[2]
    hd = di // n_heads
    alpha = (2 * depth) ** 0.25
    m0 = bsz * seq
    s0 = m0 + bsz * n_meta
    t_real = s0 + dbsz * dseq
    t_pad = _round_up(t_real, ROW_TILE)
    assert dc == d and di % dc == 0 and m0 % n_meta == 0 and s0 % (SUBLANES * dseq) == 0
    assert seq % min(SEQ_TILE, seq) == 0 and dbsz % SUBLANES == 0 and 2 * n_heads <= SUBLANES
    row = lambda a: a.reshape(1, -1)
    pad_rows = lambda a: jnp.concatenate([a, jnp.zeros((t_pad - t_real, a.shape[1]), a.dtype)], axis=0)
    flat = lambda main, meta, samp: pad_rows(jnp.concatenate([main, meta, samp], axis=0))

    x_all = flat(x_prompt.reshape(m0, d), jnp.tile(meta_tokens, (bsz, 1)), x_sample.reshape(dbsz * dseq, d))
    (x,) = _rowwise(_ln_body, t_pad, [(x_all, d, 0)], [row(ln_in_g), row(ln_in_b)], [(d, F32)], "ln_in")

    xm_blk = 2 * dc // di
    og_blk = (2 * dc + di) // di
    ga_blk = (2 * dc + 2 * di) // d
    outs = {k: [] for k in ("conv_p", "mconv_p", "c_p", "n_p", "m_p", "conv_s", "mconv_s", "c_s", "n_s", "m_s")}
    pdims, sdims = (bsz, seq, n_meta), (s0, dbsz, dseq)
    for l in range(depth):
        proj = _in_proj(x, w_in[l].astype(BF16))
        cw = (conv_w[l], row(conv_b[l]))
        cl = (row(conv_ln_g[l]), row(conv_ln_b[l]))
        a_main, a_meta, conv_p = _conv_prompt(proj, (0, 1), *cw, cl, _post_ln_silu, pdims, "conv_prompt")
        a_samp, conv_s = _conv_sample(proj, (0, 1), state_conv[l], *cw, cl, _post_ln_silu, sdims, "conv_sample")
        (ya,) = _rowwise(_matmul_body, t_pad, [(flat(a_main, a_meta, a_samp), dc, 0)],
                         [w_conv_out[l].astype(BF16)], [(d, F32)], "conv_out")
        mw = (mconv_w[l], row(mconv_b[l]))
        cm_main, cm_meta, mconv_p = _conv_prompt(proj, (xm_blk,), *mw, (), _silu, pdims, "mconv_prompt")
        cm_samp, mconv_s = _conv_sample(proj, (xm_blk,), state_mconv[l], *mw, (), _silu, sdims, "mconv_sample")
        cm = flat(cm_main, cm_meta, cm_samp)

        wif = jnp.pad(w_if[l], ((0, 0), (0, LANES - 2 * n_heads))).astype(BF16)
        wift = jnp.pad(w_if[l].T, ((0, SUBLANES - 2 * n_heads), (0, 0))).astype(BF16)
        bc = jnp.pad(b_if[l], (0, LANES - 2 * n_heads)).reshape(1, LANES)
        br = jnp.pad(b_if[l], (0, SUBLANES - 2 * n_heads)).reshape(SUBLANES, 1)
        q, k, v, gcol, grow = _qkv(cm, proj, xm_blk, _block_diag(w_q[l]), _block_diag(w_k[l]), _block_diag(w_v[l]),
                                   wif, wift, bc, br)
        grow_meta = grow[:, m0:s0].reshape(SUBLANES, bsz, n_meta).transpose(1, 0, 2)
        grow_samp = grow[:, s0:t_real].reshape(SUBLANES, dbsz, dseq).transpose(1, 0, 2)
        hh_meta, hh_main, c_p, n_p, m_p = _mlstm_prompt(q, k, v, gcol, grow, grow_meta, pdims, n_heads)
        hh_samp, c_s, n_s, m_s = _mlstm_sample(
            q, k, v, gcol, grow_samp, state_C[l].reshape(dbsz * n_heads, hd, hd), state_n[l].reshape(dbsz * n_heads, 1, hd),
            jnp.broadcast_to(state_m[l].reshape(dbsz * n_heads, 1, 1), (dbsz * n_heads, 1, LANES)), sdims, n_heads)
        (yb,) = _rowwise(_down_body, t_pad, [(flat(hh_main, hh_meta, hh_samp), di, 0), (cm, di, 0), (proj, di, og_blk)],
                         [row(mh_ln_g[l]), row(skip[l]), w_m_down[l].astype(BF16)], [(d, F32)], "down")
        (h1,) = _rowwise(functools.partial(_merge_body, alpha), t_pad,
                         [(ya, d, 0), (yb, d, 0), (proj, d, ga_blk), (proj, d, ga_blk + 1), (x, d, 0)],
                         [w_o[l].astype(BF16), row(ln1_g[l]), row(ln1_b[l])], [(d, F32)], "merge")
        x = _moe(alpha, h1, w_router[l].T.astype(BF16), router_bias[l].reshape(-1, 1), w_e1[l], w_e3[l], w_e2[l],
                 w_s1[l].astype(BF16), w_s3[l].astype(BF16), w_s2[l].astype(BF16), row(ln2_g[l]), row(ln2_b[l]))

        outs["conv_p"].append(conv_p)
        outs["mconv_p"].append(mconv_p)
        outs["c_p"].append(c_p.reshape(bsz, n_heads, hd, hd))
        outs["n_p"].append(n_p.reshape(bsz, n_heads, hd))
        outs["m_p"].append(m_p[:, 0, 0].reshape(bsz, n_heads))
        outs["conv_s"].append(conv_s)
        outs["mconv_s"].append(mconv_s)
        outs["c_s"].append(c_s.reshape(dbsz, n_heads, hd, hd))
        outs["n_s"].append(n_s.reshape(dbsz, n_heads, hd))
        outs["m_s"].append(m_s[:, 0, 0].reshape(dbsz, n_heads))

    y_prompt = x[:m0].reshape(bsz, seq, d)
    y_sample = x[s0:t_real].reshape(dbsz, dseq, d)
    st = {k: jnp.stack(v) for k, v in outs.items()}
    return (y_prompt, y_sample, st["conv_p"], st["mconv_p"], st["c_p"], st["n_p"], st["m_p"],
            st["conv_s"], st["mconv_s"], st["c_s"], st["n_s"], st["m_s"])
```

```python
import functools

import jax
import jax.numpy as jnp
from jax import lax
from jax.experimental import pallas as pl
from jax.experimental.pallas import tpu as pltpu

F32 = jnp.float32
BF16 = jnp.bfloat16
I32 = jnp.int32

LN_EPS = 1e-5
TOP_K = 8
N_GROUPS = 8
TOPK_GROUPS = 4
ROUTE_SCALE = 2.5

SUBLANES = 8
LANES = 128
MXU_DIM = 256
ROW_TILE = 512
SEQ_TILE = 256
EXPERT_ROWS = 256
VMEM_LIMIT = 48 * 1024 * 1024


def _round_up(x, m):
    return (x + m - 1) // m * m


def _params(sem):
    return pltpu.CompilerParams(dimension_semantics=sem, vmem_limit_bytes=VMEM_LIMIT)


def _ln(x, g, b):
    mu = jnp.mean(x, axis=-1, keepdims=True)
    xc = x - mu
    var = jnp.mean(xc * xc, axis=-1, keepdims=True)
    return xc * lax.rsqrt(var + LN_EPS) * g + b


def _silu(x):
    return x * jax.nn.sigmoid(x)


def _log_sigmoid(x):
    return jnp.minimum(x, 0.0) - jnp.log1p(jnp.exp(-jnp.abs(x)))


def _dot(a, b):
    return jnp.dot(a, b, preferred_element_type=F32)


def _dot_nt(a, b):
    return lax.dot_general(a, b, (((1,), (1,)), ((), ())), preferred_element_type=F32)


def _dot_tn(a, b):
    return lax.dot_general(a, b, (((0,), (0,)), ((), ())), preferred_element_type=F32)


class _Layer:
    def __init__(self, array, index):
        self.array, self.index = array, index


def _full_spec(a):
    if isinstance(a, _Layer):
        nd, l = a.array.ndim, a.index
        return pl.BlockSpec((1,) + a.array.shape[1:], lambda *_: (l,) + (0,) * (nd - 1))
    nd = a.ndim
    return pl.BlockSpec(a.shape, lambda *_: (0,) * nd)


def _arr(a):
    return a.array if isinstance(a, _Layer) else a


def _rowwise(body, rows, ins, consts, outs, name):
    tm = ROW_TILE
    in_specs = [pl.BlockSpec((tm, w), functools.partial(lambda i, cb: (i, cb), cb=cb)) for _, w, cb in ins]
    in_specs += [_full_spec(c) for c in consts]
    out_specs = [pl.BlockSpec((tm, w), lambda i: (i, 0)) for w, _ in outs]
    out_shape = [jax.ShapeDtypeStruct((rows, w), dt) for w, dt in outs]
    return pl.pallas_call(
        body, out_shape=out_shape, grid=(rows // tm,), in_specs=in_specs, out_specs=out_specs,
        compiler_params=_params(("arbitrary",)), name=name,
    )(*[a for a, _, _ in ins], *[_arr(c) for c in consts])


def _ln_body(x_ref, g_ref, b_ref, o_ref):
    o_ref[...] = _ln(x_ref[...], g_ref[...], b_ref[...])


def _in_proj_body(x_ref, w_ref, o_ref):
    o_ref[...] = _dot(x_ref[...].astype(BF16), w_ref[0])


def _in_proj(x, w, l):
    rows, d = x.shape
    n = w.shape[2]
    tm, tn = ROW_TILE, 2048
    return pl.pallas_call(
        _in_proj_body, out_shape=jax.ShapeDtypeStruct((rows, n), F32), grid=(n // tn, rows // tm),
        in_specs=[pl.BlockSpec((tm, d), lambda j, i: (i, 0)), pl.BlockSpec((1, d, tn), lambda j, i: (l, 0, j))],
        out_specs=pl.BlockSpec((tm, tn), lambda j, i: (i, j)),
        compiler_params=_params(("arbitrary", "arbitrary")), name="in_proj",
    )(x, w)


def _matmul_body(x_ref, w_ref, o_ref):
    o_ref[...] = _dot(x_ref[...].astype(BF16), w_ref[0])


def _qkv_body(cm_ref, xm_ref, wq_ref, wk_ref, wv_ref, wif_ref, wift_ref, bc_ref, br_ref,
              q_ref, k_ref, v_ref, gc_ref, gr_ref):
    di = cm_ref.shape[1]
    nb = di // MXU_DIM
    gc = jnp.zeros(gc_ref.shape, F32) + bc_ref[...]
    gr = jnp.zeros(gr_ref.shape, F32) + br_ref[...]
    for j in range(nb):
        sl = slice(j * MXU_DIM, (j + 1) * MXU_DIM)
        cmj = cm_ref[:, sl].astype(BF16)
        xmj = xm_ref[:, sl].astype(BF16)
        for o_ref, src, w_ref, seg in ((q_ref, cmj, wq_ref, 0), (k_ref, cmj, wk_ref, 1), (v_ref, xmj, wv_ref, 2)):
            y = _dot(src, w_ref[j])
            o_ref[:, sl] = y
            yb = y.astype(BF16)
            ws = slice(seg * di + j * MXU_DIM, seg * di + (j + 1) * MXU_DIM)
            gc = gc + _dot(yb, wif_ref[ws, :])
            gr = gr + _dot_nt(wift_ref[:, ws], yb)
    gc_ref[...] = gc
    gr_ref[...] = gr


def _qkv(cm, proj, xm_block, wq, wk, wv, wif, wift, bc, br):
    rows, di = cm.shape
    tm = ROW_TILE
    row_spec = lambda w, cb: pl.BlockSpec((tm, w), functools.partial(lambda i, cb: (i, cb), cb=cb))
    consts = (wq, wk, wv, wif, wift, bc, br)
    return pl.pallas_call(
        _qkv_body,
        out_shape=[jax.ShapeDtypeStruct((rows, di), F32)] * 3
        + [jax.ShapeDtypeStruct((rows, LANES), F32), jax.ShapeDtypeStruct((SUBLANES, rows), F32)],
        grid=(rows // tm,),
        in_specs=[row_spec(di, 0), row_spec(di, xm_block)] + [_full_spec(c) for c in consts],
        out_specs=[row_spec(di, 0)] * 3 + [row_spec(LANES, 0), pl.BlockSpec((SUBLANES, tm), lambda i: (0, i))],
        compiler_params=_params(("arbitrary",)), name="qkv_gates",
    )(cm, proj, *consts)


def _down_body(hh_ref, cm_ref, og_ref, g_ref, skip_ref, w_ref, o_ref):
    hb = (hh_ref[...] * g_ref[...] + skip_ref[...] * cm_ref[...]) * jax.nn.sigmoid(og_ref[...])
    o_ref[...] = _dot(hb.astype(BF16), w_ref[0])


def _merge_body(alpha, ya_ref, yb_ref, ga_ref, gb_ref, x_ref, w_ref, g_ref, b_ref, o_ref):
    merged = jax.nn.sigmoid(ga_ref[...]) * ya_ref[...] + jax.nn.sigmoid(gb_ref[...]) * yb_ref[...]
    o = _dot(merged.astype(BF16), w_ref[0])
    o_ref[...] = _ln(alpha * x_ref[...] + o, g_ref[...], b_ref[...])


def _glu_pre(a, b):
    return a * jax.nn.sigmoid(b)


def _conv_taps(k_taps):
    hp = _round_up(k_taps - 1, SUBLANES)
    return hp, hp - (k_taps - 1)


def _conv_window(s_ref, w_ref, bias, n, k_taps):
    _, off = _conv_taps(k_taps)
    acc = jnp.zeros((n, s_ref.shape[1]), F32) + bias
    for k in range(k_taps):
        acc = acc + w_ref[k:k + 1, :] * s_ref[k + off:k + off + n, :]
    return acc


def _conv_long(s_ref, sh_ref, w_ref, bias, out_ref, post, n, k_taps):
    hp, off = _conv_taps(k_taps)
    c = s_ref.shape[1]
    shifts = sorted({(k + off) % SUBLANES for k in range(k_taps)} - {0})
    span = n + hp - SUBLANES
    for s in shifts:
        sh_ref[s, 0:span, :] = s_ref[s:s + span, :]
    rg = SUBLANES * max(1, min(4, 32 * 1024 // (SUBLANES * c)))
    while n % rg:
        rg //= 2

    def body(i, carry):
        t0 = pl.multiple_of(i * rg, rg)
        acc = jnp.zeros((rg, c), F32) + bias
        for k in range(k_taps):
            g, s = divmod(k + off, SUBLANES)
            if s == 0:
                x = s_ref[pl.ds(t0 + SUBLANES * g, rg), :]
            else:
                x = sh_ref[s, pl.ds(t0 + SUBLANES * g, rg), :]
            acc = acc + w_ref[k:k + 1, :] * x
        out_ref[pl.ds(t0, rg), :] = post(acc)
        return carry

    lax.fori_loop(0, n // rg, body, 0)


def _conv_prompt_body(n_pre, k_taps, post, bsz, *refs):
    main = refs[:n_pre]
    meta = refs[n_pre:2 * n_pre]
    w_ref, b_ref = refs[2 * n_pre:2 * n_pre + 2]
    extra = refs[2 * n_pre + 2:-5]
    out_main, out_meta, state_ref, s_ref, sh_ref = refs[-5:]
    pre = _glu_pre if n_pre == 2 else (lambda a: a)
    post_fn = functools.partial(post, *[e[...] for e in extra]) if extra else post
    hp, off = _conv_taps(k_taps)
    n_meta, tl = out_meta.shape[0], out_main.shape[0]
    c = pl.program_id(1)
    real = pl.program_id(0) < bsz
    bias = b_ref[...]

    @pl.when(jnp.logical_and(real, c == 0))
    def _():
        s_ref[0:hp, :] = jnp.zeros((hp, s_ref.shape[1]), F32)
        s_ref[hp:hp + n_meta, :] = pre(*[r[...] for r in meta])
        out_meta[...] = post_fn(_conv_window(s_ref, w_ref, bias, n_meta, k_taps))
        s_ref[0:hp, :] = s_ref[n_meta:n_meta + hp, :]

    @pl.when(jnp.logical_and(real, c > 0))
    def _():
        s_ref[hp:hp + tl, :] = pre(*[r[...] for r in main])
        _conv_long(s_ref, sh_ref, w_ref, bias, out_main, post_fn, tl, k_taps)
        s_ref[0:hp, :] = s_ref[tl:tl + hp, :]

    @pl.when(jnp.logical_and(real, c == pl.num_programs(1) - 1))
    def _():
        state_ref[0] = s_ref[off:hp, :]

    @pl.when(jnp.logical_and(jnp.logical_not(real), c > 0))
    def _():
        out_main[...] = jnp.zeros(out_main.shape, F32)


def _prompt_rows(bsz, seq, t_pad):
    tl = min(SEQ_TILE, seq)
    nc = seq // tl
    blocks = t_pad // tl
    extra = -(-(blocks - bsz * nc) // nc)
    block = lambda bi, ci: jnp.minimum(bi * nc + jnp.maximum(ci - 1, 0), blocks - 1)
    return tl, nc, extra, block


def _conv_prompt(proj, col_blocks, w, b, extra, post, dims, name):
    bsz, seq, n_meta, t_pad = dims
    k_taps, c = w.shape
    hp, _ = _conv_taps(k_taps)
    tl, nc, n_extra, block = _prompt_rows(bsz, seq, t_pad)
    meta_blk0 = bsz * seq // n_meta
    n_pre = len(col_blocks)
    seq_i = lambda bi: jnp.minimum(bi, bsz - 1)
    main_map = lambda cb: (lambda bi, ci: (block(bi, ci), cb))
    meta_map = lambda cb: (lambda bi, ci: (meta_blk0 + seq_i(bi), cb))
    in_specs = [pl.BlockSpec((tl, c), main_map(cb)) for cb in col_blocks]
    in_specs += [pl.BlockSpec((n_meta, c), meta_map(cb)) for cb in col_blocks]
    consts = (w, b) + tuple(extra)
    in_specs += [_full_spec(a) for a in consts]
    return pl.pallas_call(
        functools.partial(_conv_prompt_body, n_pre, k_taps, post, bsz),
        out_shape=[jax.ShapeDtypeStruct((t_pad, c), F32), jax.ShapeDtypeStruct((bsz * n_meta, c), F32),
                   jax.ShapeDtypeStruct((bsz, k_taps - 1, c), F32)],
        grid=(bsz + n_extra, 1 + nc), in_specs=in_specs,
        out_specs=[pl.BlockSpec((tl, c), main_map(0)), pl.BlockSpec((n_meta, c), lambda bi, ci: (seq_i(bi), 0)),
                   pl.BlockSpec((1, k_taps - 1, c), lambda bi, ci: (seq_i(bi), 0, 0))],
        scratch_shapes=[pltpu.VMEM((hp + tl, c), F32), pltpu.VMEM((SUBLANES, hp + tl, c), F32)],
        compiler_params=_params(("arbitrary", "arbitrary")), name=name,
    )(*([proj] * (2 * n_pre)), *consts)


def _conv_sample_body(n_pre, k_taps, post, seqs, *refs):
    rows = refs[:n_pre]
    hist_ref, w_ref, b_ref = refs[n_pre:n_pre + 3]
    extra = refs[n_pre + 3:-3]
    out_ref, state_ref, s_ref = refs[-3:]
    pre = _glu_pre if n_pre == 2 else (lambda a: a)
    post_fn = functools.partial(post, *[e[...] for e in extra]) if extra else post
    hp, off = _conv_taps(k_taps)
    n = out_ref.shape[0] // seqs
    bias = b_ref[...]

    def body(j, carry):
        r0 = pl.multiple_of(j * n, n)
        s_ref[off:hp, :] = hist_ref[j]
        s_ref[hp:hp + n, :] = pre(*[r[pl.ds(r0, n), :] for r in rows])
        out_ref[pl.ds(r0, n), :] = post_fn(_conv_window(s_ref, w_ref, bias, n, k_taps))
        state_ref[j] = s_ref[off + n:hp + n, :]
        return carry

    lax.fori_loop(0, seqs, body, 0)


def _conv_sample(proj, col_blocks, hist, layer, w, b, extra, post, dims, name):
    row0, dbsz, n = dims
    k_taps, c = w.shape
    hp, _ = _conv_taps(k_taps)
    seqs = min(SUBLANES, dbsz)
    blk0 = row0 // (seqs * n)
    hist0 = layer * (dbsz // seqs)
    n_pre = len(col_blocks)
    consts = (w, b) + tuple(extra)
    in_specs = [pl.BlockSpec((seqs * n, c), functools.partial(lambda i, cb: (blk0 + i, cb), cb=cb)) for cb in col_blocks]
    in_specs += [pl.BlockSpec((seqs, k_taps - 1, c), lambda i: (hist0 + i, 0, 0))]
    in_specs += [_full_spec(a) for a in consts]
    return pl.pallas_call(
        functools.partial(_conv_sample_body, n_pre, k_taps, post, seqs),
        out_shape=[jax.ShapeDtypeStruct((dbsz * n, c), F32), jax.ShapeDtypeStruct((dbsz, k_taps - 1, c), F32)],
        grid=(dbsz // seqs,), in_specs=in_specs,
        out_specs=[pl.BlockSpec((seqs * n, c), lambda i: (i, 0)), pl.BlockSpec((seqs, k_taps - 1, c), lambda i: (i, 0, 0))],
        scratch_shapes=[pltpu.VMEM((hp + n, c), F32)],
        compiler_params=_params(("arbitrary",)), name=name,
    )(*([proj] * n_pre), hist, *consts)


def _post_ln_silu(g, b, acc):
    return _silu(_ln(acc, g, b))


def _mlstm_chunk(gc, gr, q, k, v, c0, n0, m0, hsel, n_heads):
    seq, hd = q.shape
    lane = lax.broadcasted_iota(I32, gc.shape, 1)
    sub = lax.broadcasted_iota(I32, gr.shape, 0)
    pick_c = lambda j: jnp.sum(jnp.where(lane == j, gc, 0.0), axis=1, keepdims=True)
    pick_r = lambda j: jnp.sum(jnp.where(sub == j, gr, 0.0), axis=0, keepdims=True)
    ig_c, lf_c = pick_c(hsel), _log_sigmoid(pick_c(hsel + n_heads))
    ig_r, lf_r = pick_r(hsel), _log_sigmoid(pick_r(hsel + n_heads))
    ti = lax.broadcasted_iota(I32, (seq, seq), 0)
    si = lax.broadcasted_iota(I32, (seq, seq), 1)
    causal = si <= ti
    b_c = jnp.sum(jnp.where(causal, lf_r, 0.0), axis=1, keepdims=True)
    b_r = jnp.sum(jnp.where(ti <= si, lf_c, 0.0), axis=0, keepdims=True)
    log_w = jnp.where(causal, b_c - b_r + ig_r, -jnp.inf)
    log_s = b_c + m0
    m_c = jnp.maximum(log_s, jnp.max(log_w, axis=1, keepdims=True))
    w = jnp.exp(log_w - m_c)
    s_c = jnp.exp(log_s - m_c)
    ks = k * (hd ** -0.5)
    qb, kb, vb = q.astype(BF16), ks.astype(BF16), v.astype(BF16)
    qk = _dot_nt(qb, kb) * w
    num = s_c * _dot(qb, c0.astype(BF16)) + _dot(qk.astype(BF16), vb)
    den = s_c * jnp.sum(q * n0, axis=1, keepdims=True) + jnp.sum(qk, axis=1, keepdims=True)
    hval = num / jnp.maximum(jnp.abs(den), jnp.exp(-m_c))
    mu = jnp.mean(hval, axis=1, keepdims=True)
    hc = hval - mu
    var = jnp.mean(hc * hc, axis=1, keepdims=True)
    hh = hc * lax.rsqrt(var + LN_EPS)
    m_end = m_c[seq - 1:seq, :]
    b_last = b_c[seq - 1:seq, :]
    w_end = jnp.exp(b_last - b_c + ig_c - m_end)
    s_end = jnp.exp(b_last + m0 - m_end)
    kw = ks * w_end
    c1 = s_end * c0 + _dot_tn(kw.astype(BF16), vb)
    n1 = s_end * n0 + jnp.sum(kw, axis=0, keepdims=True)
    return hh, c1, n1, m_end


def _mlstm_prompt_body(n_heads, bsz, gcm_ref, grm_ref, qm_ref, km_ref, vm_ref, gc_ref, gr_ref, q_ref, k_ref, v_ref,
                       hm_ref, h_ref, c_out, n_out, m_out, c_s, n_s, m_s):
    hsel = pl.program_id(1)
    c = pl.program_id(2)
    real = pl.program_id(0) < bsz

    def step(gc, gr, q, k, v, out):
        hh, c1, n1, m1 = _mlstm_chunk(gc, gr, q, k, v, c_s[...], n_s[...], m_s[0:1, 0:1], hsel, n_heads)
        out[...] = hh
        c_s[...] = c1
        n_s[...] = n1
        m_s[...] = jnp.broadcast_to(m1, m_s.shape)

    @pl.when(jnp.logical_and(real, c == 0))
    def _():
        c_s[...] = jnp.zeros(c_s.shape, F32)
        n_s[...] = jnp.zeros(n_s.shape, F32)
        m_s[...] = jnp.zeros(m_s.shape, F32)
        step(gcm_ref[...], grm_ref[0], qm_ref[...], km_ref[...], vm_ref[...], hm_ref)

    @pl.when(jnp.logical_and(real, c > 0))
    def _():
        step(gc_ref[...], gr_ref[...], q_ref[...], k_ref[...], v_ref[...], h_ref)

    @pl.when(jnp.logical_and(real, c == pl.num_programs(2) - 1))
    def _():
        c_out[0] = c_s[...]
        n_out[0] = n_s[...]
        m_out[0] = m_s[...]

    @pl.when(jnp.logical_and(jnp.logical_not(real), c > 0))
    def _():
        h_ref[...] = jnp.zeros(h_ref.shape, F32)


def _mlstm_prompt(q, k, v, gcol, grow, grow_meta, dims, n_heads):
    bsz, seq, n_meta, t_pad = dims
    hd = q.shape[1] // n_heads
    tl, nc, n_extra, block = _prompt_rows(bsz, seq, t_pad)
    meta_blk0 = bsz * seq // n_meta
    main_r = lambda bi, hi, ci: block(bi, ci)
    real = lambda bi: bi < bsz
    seq_i = lambda bi: jnp.minimum(bi, bsz - 1)
    head_i = lambda bi, hi: jnp.where(real(bi), hi, n_heads - 1)
    qkv_main = pl.BlockSpec((tl, hd), lambda bi, hi, ci: (main_r(bi, hi, ci), hi))
    qkv_meta = pl.BlockSpec((n_meta, hd), lambda bi, hi, ci: (meta_blk0 + seq_i(bi), hi))
    st = lambda shape: pl.BlockSpec((1,) + shape, lambda bi, hi, ci: (seq_i(bi) * n_heads + head_i(bi, hi), 0, 0))
    return pl.pallas_call(
        functools.partial(_mlstm_prompt_body, n_heads, bsz),
        out_shape=[jax.ShapeDtypeStruct((bsz * n_meta, q.shape[1]), F32), jax.ShapeDtypeStruct((t_pad, q.shape[1]), F32),
                   jax.ShapeDtypeStruct((bsz * n_heads, hd, hd), F32), jax.ShapeDtypeStruct((bsz * n_heads, 1, hd), F32),
                   jax.ShapeDtypeStruct((bsz * n_heads, 1, LANES), F32)],
        grid=(bsz + n_extra, n_heads, 1 + nc),
        in_specs=[pl.BlockSpec((n_meta, LANES), lambda bi, hi, ci: (meta_blk0 + seq_i(bi), 0)),
                  pl.BlockSpec((1, SUBLANES, n_meta), lambda bi, hi, ci: (seq_i(bi), 0, 0)),
                  qkv_meta, qkv_meta, qkv_meta,
                  pl.BlockSpec((tl, LANES), lambda bi, hi, ci: (main_r(bi, hi, ci), 0)),
                  pl.BlockSpec((SUBLANES, tl), lambda bi, hi, ci: (0, main_r(bi, hi, ci))),
                  qkv_main, qkv_main, qkv_main],
        out_specs=[pl.BlockSpec((n_meta, hd), lambda bi, hi, ci: (seq_i(bi), head_i(bi, hi))), qkv_main,
                   st((hd, hd)), st((1, hd)), st((1, LANES))],
        scratch_shapes=[pltpu.VMEM((hd, hd), F32), pltpu.VMEM((1, hd), F32), pltpu.VMEM((1, LANES), F32)],
        compiler_params=_params(("arbitrary",) * 3), name="mlstm_prompt",
    )(gcol, grow_meta, q, k, v, gcol, grow, q, k, v)


def _mlstm_sample_body(n_heads, has_acc, gc_ref, gr_ref, q_ref, k_ref, v_ref, c0_ref, n0_ref, m0_ref, *refs):
    h_ref, c_out, n_out, m_out = refs[1:] if has_acc else refs
    hd = q_ref.shape[1] // n_heads
    gc, gr = gc_ref[...], gr_ref[0]
    for h in range(n_heads):
        cols = slice(h * hd, (h + 1) * hd)
        hh, c1, n1, m1 = _mlstm_chunk(gc, gr, q_ref[:, cols], k_ref[:, cols], v_ref[:, cols],
                                      c0_ref[0, h], n0_ref[0, h:h + 1, :], m0_ref[0, h:h + 1, 0:1], h, n_heads)
        h_ref[:, cols] = hh
        c_out[0, h] = c1
        n_out[0, h:h + 1, :] = n1
        m_out[0, h:h + 1, :] = jnp.broadcast_to(m1, (1, m_out.shape[2]))


def _mlstm_sample(q, k, v, gcol, grow_s, c0_all, c_acc, layer, n0, m0, dims, n_heads):
    row0, dbsz, n = dims
    di = q.shape[1]
    hd = di // n_heads
    blk0 = row0 // n
    st0 = layer * dbsz
    qkv = pl.BlockSpec((n, di), lambda bi: (blk0 + bi, 0))
    c_spec = pl.BlockSpec((1, n_heads, hd, hd), lambda bi: (st0 + bi, 0, 0, 0))
    small = lambda w: pl.BlockSpec((1, n_heads, w), lambda bi: (bi, 0, 0))
    has_acc = c_acc is not None
    operands = (gcol, grow_s, q, k, v, c0_all, n0, m0) + ((c_acc,) if has_acc else ())
    return pl.pallas_call(
        functools.partial(_mlstm_sample_body, n_heads, has_acc),
        out_shape=[jax.ShapeDtypeStruct((dbsz * n, di), F32), jax.ShapeDtypeStruct(c0_all.shape, F32),
                   jax.ShapeDtypeStruct((dbsz, n_heads, hd), F32), jax.ShapeDtypeStruct((dbsz, n_heads, LANES), F32)],
        grid=(dbsz,),
        in_specs=[pl.BlockSpec((n, LANES), lambda bi: (blk0 + bi, 0)),
                  pl.BlockSpec((1, SUBLANES, n), lambda bi: (bi, 0, 0)),
                  qkv, qkv, qkv, c_spec, small(hd), small(LANES)]
        + ([pl.BlockSpec(memory_space=pl.ANY)] if has_acc else []),
        out_specs=[pl.BlockSpec((n, di), lambda bi: (bi, 0)), c_spec, small(hd), small(LANES)],
        input_output_aliases={8: 1} if has_acc else {},
        compiler_params=_params(("arbitrary",)), name="mlstm_sample",
    )(*operands)


def _stack_rows(rows, n):
    sub = lax.broadcasted_iota(I32, (len(rows), n), 0)
    out = jnp.zeros((len(rows), n), rows[0].dtype)
    for j, r in enumerate(rows):
        out = jnp.where(sub == j, r, out)
    return out


def _router_body(h_ref, wt_ref, bias_ref, te_ref, wk_ref):
    n_exp = wt_ref.shape[0]
    gsz = n_exp // N_GROUPS
    tm = h_ref.shape[0]
    logits = _dot_nt(wt_ref[...], h_ref[...].astype(BF16))
    scores = jax.nn.sigmoid(logits)
    choice = scores + bias_ref[...]
    sub = lax.broadcasted_iota(I32, (gsz, tm), 0)
    neg = -jnp.inf

    def first_max(x):
        m = jnp.max(x, axis=0, keepdims=True)
        idx = jnp.min(jnp.where(x == m, sub, gsz), axis=0, keepdims=True)
        return m, idx

    sc = [scores[g * gsz:(g + 1) * gsz, :] for g in range(N_GROUPS)]
    ch = [choice[g * gsz:(g + 1) * gsz, :] for g in range(N_GROUPS)]
    gs = []
    for x in ch:
        m1, i1 = first_max(x)
        m2, _ = first_max(jnp.where(sub == i1, neg, x))
        gs.append(m1 + m2)
    y = _stack_rows(gs, tm)
    gsub = lax.broadcasted_iota(I32, (N_GROUPS, tm), 0)
    gsel = jnp.zeros((N_GROUPS, tm), jnp.bool_)
    for _ in range(TOPK_GROUPS):
        m = jnp.max(y, axis=0, keepdims=True)
        idx = jnp.min(jnp.where(y == m, gsub, N_GROUPS), axis=0, keepdims=True)
        hit = gsub == idx
        gsel = jnp.logical_or(gsel, hit)
        y = jnp.where(hit, neg, y)
    gself = gsel.astype(F32)
    ch = [jnp.where(gself[g:g + 1, :] > 0.0, ch[g], neg) for g in range(N_GROUPS)]
    eid = [sub + g * gsz for g in range(N_GROUPS)]
    picks, pick_w = [], []
    for _ in range(TOP_K):
        m = functools.reduce(jnp.maximum, [jnp.max(x, axis=0, keepdims=True) for x in ch])
        idx = functools.reduce(jnp.minimum, [jnp.min(jnp.where(x == m, e, n_exp), axis=0, keepdims=True)
                                             for x, e in zip(ch, eid)])
        hits = [e == idx for e in eid]
        pick_w.append(functools.reduce(jnp.add, [jnp.sum(jnp.where(h, s, 0.0), axis=0, keepdims=True)
                                                 for h, s in zip(hits, sc)]))
        ch = [jnp.where(h, neg, x) for h, x in zip(hits, ch)]
        picks.append(idx)
    total = functools.reduce(jnp.add, pick_w)
    te_ref[...] = _stack_rows(picks, tm)
    wk_ref[...] = _stack_rows([w / total * ROUTE_SCALE for w in pick_w], tm)


def _router(h, wt, bias):
    rows, d = h.shape
    tm = ROW_TILE
    return pl.pallas_call(
        _router_body,
        out_shape=[jax.ShapeDtypeStruct((TOP_K, rows), I32), jax.ShapeDtypeStruct((TOP_K, rows), F32)],
        grid=(rows // tm,),
        in_specs=[pl.BlockSpec((tm, d), lambda i: (i, 0)), _full_spec(wt), _full_spec(bias)],
        out_specs=[pl.BlockSpec((TOP_K, tm), lambda i: (0, i))] * 2,
        compiler_params=_params(("arbitrary",)), name="router",
    )(h, wt, bias)


def _dispatch_body(n_blocks, te_ref, dest_ref, blke_ref, used_ref, start_ref, cnt_ref, cnt_s, start_s, carry_s):
    n_exp = cnt_s.shape[0]
    tw = te_ref.shape[1]
    phase, i = pl.program_id(0), pl.program_id(1)
    te = te_ref[...]
    eio = lax.broadcasted_iota(I32, (n_exp, tw), 0)
    sel = functools.reduce(jnp.logical_or, [eio == te[k:k + 1, :] for k in range(TOP_K)])
    self = sel.astype(F32)

    @pl.when(jnp.logical_and(phase == 0, i == 0))
    def _():
        cnt_s[...] = jnp.zeros(cnt_s.shape, F32)

    @pl.when(phase == 0)
    def _():
        cnt_s[...] += jnp.sum(self, axis=1, keepdims=True)

    @pl.when(jnp.logical_and(phase == 1, i == 0))
    def _():
        cnt = cnt_s[...]
        nblk = jnp.floor((cnt + (EXPERT_ROWS - 1)) * (1.0 / EXPERT_ROWS))
        r = lax.broadcasted_iota(I32, (n_exp, n_exp), 0)
        c = lax.broadcasted_iota(I32, (n_exp, n_exp), 1)
        lower = (c < r).astype(BF16)
        blk_start = _dot(lower, nblk.astype(BF16))
        blk_end = blk_start + nblk
        start_s[...] = blk_start * EXPERT_ROWS
        carry_s[...] = jnp.zeros(carry_s.shape, F32)
        start_ref[...] = (blk_start * EXPERT_ROWS).astype(I32)
        cnt_ref[...] = cnt.astype(I32)
        used_ref[...] = jnp.sum(nblk, axis=0, keepdims=True).astype(I32)
        jio = lax.broadcasted_iota(I32, (n_exp, blke_ref.shape[1]), 1).astype(F32)
        be = jnp.sum((blk_end[:, 0:1] <= jio).astype(F32), axis=0, keepdims=True)
        blke_ref[...] = jnp.minimum(be, n_exp - 1.0).astype(I32)

    @pl.when(phase == 1)
    def _():
        r = lax.broadcasted_iota(I32, (tw, tw), 0)
        c = lax.broadcasted_iota(I32, (tw, tw), 1)
        upper = (r <= c).astype(BF16)
        incl = _dot(self.astype(BF16), upper)
        dest = start_s[:, 0:1] + carry_s[:, 0:1] + incl - self
        carry_s[...] += jnp.sum(self, axis=1, keepdims=True)
        rows = [jnp.sum(jnp.where(eio == te[k:k + 1, :], dest, 0.0), axis=0, keepdims=True) for k in range(TOP_K)]
        dest_ref[...] = _stack_rows(rows, tw).astype(I32)


def _dispatch(te, n_exp, n_blocks):
    rows = te.shape[1]
    tw = ROW_TILE
    nbp = _round_up(n_blocks, LANES)
    small = lambda shape: pl.BlockSpec(shape, lambda p, i: (0, 0))
    return pl.pallas_call(
        functools.partial(_dispatch_body, n_blocks),
        out_shape=[jax.ShapeDtypeStruct((TOP_K, rows), I32), jax.ShapeDtypeStruct((1, nbp), I32),
                   jax.ShapeDtypeStruct((1, LANES), I32), jax.ShapeDtypeStruct((n_exp, LANES), I32),
                   jax.ShapeDtypeStruct((n_exp, LANES), I32)],
        grid=(2, rows // tw),
        in_specs=[pl.BlockSpec((TOP_K, tw), lambda p, i: (0, i))],
        out_specs=[pl.BlockSpec((TOP_K, tw), lambda p, i: (0, i * p)), small((1, nbp)), small((1, LANES)),
                   small((n_exp, LANES)), small((n_exp, LANES))],
        scratch_shapes=[pltpu.VMEM((n_exp, LANES), F32)] * 3,
        compiler_params=_params(("arbitrary", "arbitrary")), name="dispatch",
    )(te)


def _scatter_body(start_ref, cnt_ref, used_ref, dest_ref, h_ref, xs_ref, zero_ref, sem):
    tm = h_ref.shape[0]
    row_copy = lambda t, r: pltpu.make_async_copy(h_ref.at[pl.ds(t, 1)], xs_ref.at[pl.ds(r, 1)], sem)

    def issue(t, carry):
        for k in range(TOP_K):
            row_copy(t, dest_ref[k, t]).start(priority=k % 2)
        return carry

    lax.fori_loop(0, tm, issue, 0)
    for k in range(TOP_K):
        pltpu.make_async_copy(h_ref, xs_ref.at[pl.ds(0, tm)], sem).wait()

    @pl.when(pl.program_id(0) == pl.num_programs(0) - 1)
    def _():
        zero_ref[...] = jnp.zeros(zero_ref.shape, F32)
        sizes = [SUBLANES << b for b in range((EXPERT_ROWS // SUBLANES).bit_length() - 1)]

        def fill(e, carry):
            cnt = cnt_ref[e]
            pad = (-cnt) & (EXPERT_ROWS - 1)
            row = start_ref[e] + cnt
            head = jnp.minimum((-row) & (SUBLANES - 1), pad)
            for r in range(SUBLANES - 1):
                cp = pltpu.make_async_copy(zero_ref.at[pl.ds(0, 1)], xs_ref.at[pl.ds(row + r, 1)], sem)

                @pl.when(r < head)
                def _():
                    cp.start()
                    cp.wait()

            row = row + head
            rest = pad - head
            for sz in sizes:
                cp = pltpu.make_async_copy(zero_ref.at[pl.ds(0, sz)],
                                           xs_ref.at[pl.ds(pl.multiple_of(row, SUBLANES), sz)], sem)

                @pl.when((rest & sz) != 0)
                def _():
                    cp.start()
                    cp.wait()

                row = row + (rest & sz)
            return carry

        lax.fori_loop(0, start_ref.shape[0], fill, 0)

        def fill_block(j, carry):
            for r0 in range(0, EXPERT_ROWS, zero_ref.shape[0]):
                row = pl.multiple_of(j * EXPERT_ROWS + r0, SUBLANES)
                cp = pltpu.make_async_copy(zero_ref, xs_ref.at[pl.ds(row, zero_ref.shape[0])], sem)
                cp.start()
                cp.wait()
            return carry

        lax.fori_loop(used_ref[0], xs_ref.shape[0] // EXPERT_ROWS, fill_block, 0)


def _scatter(h, dest, start, cnt, used, n_blocks):
    rows, d = h.shape
    tm = ROW_TILE // 2
    grid_spec = pltpu.PrefetchScalarGridSpec(
        num_scalar_prefetch=3, grid=(rows // tm,),
        in_specs=[pl.BlockSpec((TOP_K, tm), lambda i, *_: (0, i), memory_space=pltpu.SMEM),
                  pl.BlockSpec((tm, d), lambda i, *_: (i, 0))],
        out_specs=pl.BlockSpec(memory_space=pl.ANY),
        scratch_shapes=[pltpu.VMEM((EXPERT_ROWS // 2, d), F32), pltpu.SemaphoreType.DMA],
    )
    return pl.pallas_call(
        _scatter_body, out_shape=jax.ShapeDtypeStruct((n_blocks * EXPERT_ROWS, d), F32), grid_spec=grid_spec,
        compiler_params=_params(("arbitrary",)), name="moe_scatter",
    )(start, cnt, used, dest, h)


def _expert_body(blke_ref, used_ref, xs_ref, w1_ref, w3_ref, w2_ref, ys_ref, w1_s, w3_s, w2_s):
    j = pl.program_id(0)
    prev = blke_ref[jnp.maximum(j - 1, 0)]
    live = j < used_ref[0]

    @pl.when(jnp.logical_and(live, jnp.logical_or(j == 0, blke_ref[j] != prev)))
    def _():
        w1_s[...] = w1_ref[0].astype(BF16)
        w3_s[...] = w3_ref[0].astype(BF16)
        w2_s[...] = w2_ref[0].astype(BF16)

    @pl.when(live)
    def _():
        half = EXPERT_ROWS // 2
        for r0 in (0, half):
            x = xs_ref[r0:r0 + half, :].astype(BF16)
            mid = _silu(_dot(x, w1_s[...])) * _dot(x, w3_s[...])
            ys_ref[r0:r0 + half, :] = _dot(mid.astype(BF16), w2_s[...])

    @pl.when(jnp.logical_not(live))
    def _():
        ys_ref[...] = jnp.zeros(ys_ref.shape, F32)


def _experts(xs, blke, used, w1, w3, w2, e0):
    rows, d = xs.shape
    de = w1.shape[2]
    n_blocks = rows // EXPERT_ROWS
    blk =lambda j, blke_ref, used_ref: jnp.minimum(j, used_ref[0] - 1)
    xmap = lambda j, b, u: (blk(j, b, u), 0)
    wmap = lambda j, b, u: (e0 + b[blk(j, b, u)], 0, 0)
    grid_spec = pltpu.PrefetchScalarGridSpec(
        num_scalar_prefetch=2, grid=(n_blocks,),
        in_specs=[pl.BlockSpec((EXPERT_ROWS, d), xmap), pl.BlockSpec((1, d, de), wmap),
                  pl.BlockSpec((1, d, de), wmap), pl.BlockSpec((1, de, d), wmap)],
        out_specs=pl.BlockSpec((EXPERT_ROWS, d), lambda j, b, u: (j, 0)),
        scratch_shapes=[pltpu.VMEM((d, de), BF16), pltpu.VMEM((d, de), BF16), pltpu.VMEM((de, d), BF16)],
    )
    return pl.pallas_call(
        _expert_body, out_shape=jax.ShapeDtypeStruct((rows, d), F32), grid_spec=grid_spec,
        compiler_params=_params(("arbitrary",)), name="moe_experts",
    )(blke, used, xs, w1, w3, w2)


def _combine_body(alpha, dest_ref, wk_ref, h_ref, ys_ref, ws1_ref, ws3_ref, ws2_ref, g_ref, b_ref, o_ref, buf, sem):
    tm = h_ref.shape[0]
    row_copy = lambda k, t, r: pltpu.make_async_copy(ys_ref.at[pl.ds(r, 1)], buf.at[k, pl.ds(t, 1)], sem)

    def issue(t, carry):
        for k in range(TOP_K):
            row_copy(k, t, dest_ref[k, t]).start(priority=k % 2)
        return carry

    lax.fori_loop(0, tm, issue, 0)
    h = h_ref[...]
    hb = h.astype(BF16)
    f = _dot((_silu(_dot(hb, ws1_ref[...])) * _dot(hb, ws3_ref[...])).astype(BF16), ws2_ref[...])
    for k in range(TOP_K):
        pltpu.make_async_copy(ys_ref.at[pl.ds(0, tm)], buf.at[k], sem).wait()
    for k in range(TOP_K):
        f = f + wk_ref[:, k:k + 1] * buf[k]
    o_ref[...] = _ln(alpha * h + f, g_ref[...], b_ref[...])


def _combine(alpha, h, ys, dest, wk_col, ws1, ws3, ws2, g, b):
    rows, d = h.shape
    tm = ROW_TILE // 4
    consts = (ws1, ws3, ws2, g, b)
    return pl.pallas_call(
        functools.partial(_combine_body, alpha), out_shape=jax.ShapeDtypeStruct((rows, d), F32), grid=(rows // tm,),
        in_specs=[pl.BlockSpec((TOP_K, tm), lambda i: (0, i), memory_space=pltpu.SMEM),
                  pl.BlockSpec((tm, TOP_K), lambda i: (i, 0)), pl.BlockSpec((tm, d), lambda i: (i, 0)),
                  pl.BlockSpec(memory_space=pl.ANY)] + [_full_spec(c) for c in consts],
        out_specs=pl.BlockSpec((tm, d), lambda i: (i, 0)),
        scratch_shapes=[pltpu.VMEM((TOP_K, tm, d), F32), pltpu.SemaphoreType.DMA],
        compiler_params=_params(("arbitrary",)), name="moe_combine",
    )(dest, wk_col, h, ys, *consts)


def _moe(alpha, h, wrt, rbias, w1, w3, w2, layer, ws1, ws3, ws2, g, b):
    rows = h.shape[0]
    n_exp = wrt.shape[0]
    n_blocks = -(-rows * TOP_K // EXPERT_ROWS) + n_exp
    te, wk = _router(h, wrt, rbias)
    dest, blke, used, start, cnt = _dispatch(te, n_exp, n_blocks)
    xs = _scatter(h, dest, start[:, 0], cnt[:, 0], used[0, :1], n_blocks)
    ys = _experts(xs, blke[0, :n_blocks], used[0, :1], w1, w3, w2, layer * n_exp)
    return _combine(alpha, h, ys, dest, wk.T, ws1, ws3, ws2, g, b)


def _block_diag(w):
    nb, qb, _ = w.shape
    per = MXU_DIM // qb
    wt = w.reshape(nb // per, per, qb, qb)
    eye = jnp.eye(per, dtype=w.dtype)
    full = wt[:, :, :, None, :] * eye[None, :, None, :, None]
    return full.reshape(nb // per, MXU_DIM, MXU_DIM).astype(BF16)


def kernel(x_prompt, x_sample, state_conv, state_mconv, state_C, state_n, state_m, meta_tokens, ln_in_g, ln_in_b, w_in, conv_w, conv_b, conv_ln_g, conv_ln_b, w_conv_out, mconv_w, mconv_b, w_q, w_k, w_v, w_if, b_if, mh_ln_g, skip, w_m_down, w_o, ln1_g, ln1_b, ln2_g, ln2_b, w_router, router_bias, w_e1, w_e3, w_e2, w_s1, w_s3, w_s2):
    bsz, seq, d = x_prompt.shape
    dbsz, dseq, _ = x_sample.shape
    depth = w_in.shape[0]
    n_meta = meta_tokens.shape[0]
    dc = conv_w.shape[2]
    di = mconv_w.shape[2]
    n_heads = state_m.shape[2]
    hd = di // n_heads
    alpha = (2 * depth) ** 0.25
    m0 = bsz * seq
    s0 = m0 + bsz * n_meta
    t_real = s0 + dbsz * dseq
    t_pad = _round_up(t_real, ROW_TILE)
    assert dc == d and di % dc == 0 and m0 % n_meta == 0 and s0 % (SUBLANES * dseq) == 0
    assert seq % min(SEQ_TILE, seq) == 0 and dbsz % SUBLANES == 0 and 2 * n_heads <= SUBLANES
    row = lambda a: a.reshape(1, -1)

    def flat(main, meta, samp):
        out = lax.dynamic_update_slice(main, meta, (m0, 0))
        return lax.dynamic_update_slice(out, samp, (s0, 0))

    x_all = jnp.concatenate([x_prompt.reshape(m0, d), jnp.tile(meta_tokens, (bsz, 1)), x_sample.reshape(dbsz * dseq, d),
                             jnp.zeros((t_pad - t_real, d), F32)], axis=0)
    (x,) = _rowwise(_ln_body, t_pad, [(x_all, d, 0)], [row(ln_in_g), row(ln_in_b)], [(d, F32)], "ln_in")

    xm_blk = 2 * dc // di
    og_blk = (2 * dc + di) // di
    ga_blk = (2 * dc + 2 * di) // d
    outs = {k: [] for k in ("conv_p", "mconv_p", "c_p", "n_p", "m_p", "conv_s", "mconv_s", "n_s", "m_s")}
    pdims, sdims = (bsz, seq, n_meta, t_pad), (s0, dbsz, dseq)
    w_in_b, w_co_b, w_md_b, w_o_b = (w.astype(BF16) for w in (w_in, w_conv_out, w_m_down, w_o))
    n_exp = w_e1.shape[1]
    we1, we3, we2 = (w.reshape((depth * n_exp,) + w.shape[2:]) for w in (w_e1, w_e3, w_e2))
    hist_c = state_conv.reshape((depth * dbsz,) + state_conv.shape[2:])
    hist_m = state_mconv.reshape((depth * dbsz,) + state_mconv.shape[2:])
    c0_all = state_C.reshape(depth * dbsz, n_heads, hd, hd)
    c_s = None
    for l in range(depth):
        proj = _in_proj(x, w_in_b, l)
        cw = (conv_w[l], row(conv_b[l]))
        cl = (row(conv_ln_g[l]), row(conv_ln_b[l]))
        a_main, a_meta, conv_p = _conv_prompt(proj, (0, 1), *cw, cl, _post_ln_silu, pdims, "conv_prompt")
        a_samp, conv_s = _conv_sample(proj, (0, 1), hist_c, l, *cw, cl, _post_ln_silu, sdims, "conv_sample")
        (ya,) = _rowwise(_matmul_body, t_pad, [(flat(a_main, a_meta, a_samp), dc, 0)],
                         [_Layer(w_co_b, l)], [(d, F32)], "conv_out")
        mw = (mconv_w[l], row(mconv_b[l]))
        cm_main, cm_meta, mconv_p = _conv_prompt(proj, (xm_blk,), *mw, (), _silu, pdims, "mconv_prompt")
        cm_samp, mconv_s = _conv_sample(proj, (xm_blk,), hist_m, l, *mw, (), _silu, sdims, "mconv_sample")
        cm = flat(cm_main, cm_meta, cm_samp)

        wif = jnp.pad(w_if[l], ((0, 0), (0, LANES - 2 * n_heads))).astype(BF16)
        wift = jnp.pad(w_if[l].T, ((0, SUBLANES - 2 * n_heads), (0, 0))).astype(BF16)
        bc = jnp.pad(b_if[l], (0, LANES - 2 * n_heads)).reshape(1, LANES)
        br = jnp.pad(b_if[l], (0, SUBLANES - 2 * n_heads)).reshape(SUBLANES, 1)
        q, k, v, gcol, grow = _qkv(cm, proj, xm_blk, _block_diag(w_q[l]), _block_diag(w_k[l]), _block_diag(w_v[l]),
                                   wif, wift, bc, br)
        grow_meta = grow[:, m0:s0].reshape(SUBLANES, bsz, n_meta).transpose(1, 0, 2)
        grow_samp = grow[:, s0:t_real].reshape(SUBLANES, dbsz, dseq).transpose(1, 0, 2)
        hh_meta, hh_main, c_p, n_p, m_p = _mlstm_prompt(q, k, v, gcol, grow, grow_meta, pdims, n_heads)
        hh_samp, c_s, n_s, m_s = _mlstm_sample(
            q, k, v, gcol, grow_samp, c0_all, c_s, l, state_n[l],
            jnp.broadcast_to(state_m[l][:, :, None], (dbsz, n_heads, LANES)), sdims, n_heads)
        (yb,) = _rowwise(_down_body, t_pad, [(flat(hh_main, hh_meta, hh_samp), di, 0), (cm, di, 0), (proj, di, og_blk)],
                         [row(mh_ln_g[l]), row(skip[l]), _Layer(w_md_b, l)], [(d, F32)], "down")
        (h1,) = _rowwise(functools.partial(_merge_body, alpha), t_pad,
                         [(ya, d, 0), (yb, d, 0), (proj, d, ga_blk), (proj, d, ga_blk + 1), (x, d, 0)],
                         [_Layer(w_o_b, l), row(ln1_g[l]), row(ln1_b[l])], [(d, F32)], "merge")
        x = _moe(alpha, h1, w_router[l].T.astype(BF16), router_bias[l].reshape(-1, 1), we1, we3, we2, l,
                 w_s1[l].astype(BF16), w_s3[l].astype(BF16), w_s2[l].astype(BF16), row(ln2_g[l]), row(ln2_b[l]))

        outs["conv_p"].append(conv_p)
        outs["mconv_p"].append(mconv_p)
        outs["c_p"].append(c_p.reshape(bsz, n_heads, hd, hd))
        outs["n_p"].append(n_p.reshape(bsz, n_heads, hd))
        outs["m_p"].append(m_p[:, 0, 0].reshape(bsz, n_heads))
        outs["conv_s"].append(conv_s)
        outs["mconv_s"].append(mconv_s)
        outs["n_s"].append(n_s)
        outs["m_s"].append(m_s[:, :, 0])

    y_prompt = x[:m0].reshape(bsz, seq, d)
    y_sample = x[s0:t_real].reshape(dbsz, dseq, d)
    st = {k: jnp.stack(v) for k, v in outs.items()}
    return (y_prompt, y_sample, st["conv_p"], st["mconv_p"], st["c_p"], st["n_p"], st["m_p"],
            st["conv_s"], st["mconv_s"], c_s.reshape(depth, dbsz, n_heads, hd, hd), st["n_s"], st["m_s"])
```

```python
import functools

import jax
import jax.numpy as jnp
from jax import lax
from jax.experimental import pallas as pl
from jax.experimental.pallas import tpu as pltpu

F32 = jnp.float32
BF16 = jnp.bfloat16
I32 = jnp.int32

LN_EPS = 1e-5
TOP_K = 8
N_GROUPS = 8
TOPK_GROUPS = 4
ROUTE_SCALE = 2.5

SUBLANES = 8
LANES = 128
MXU_DIM = 256
ROW_TILE = 512
SEQ_TILE = 256
EXPERT_ROWS = 512
VMEM_LIMIT = 48 * 1024 * 1024


def _round_up(x, m):
    return (x + m - 1) // m * m


def _params(sem):
    return pltpu.CompilerParams(dimension_semantics=sem, vmem_limit_bytes=VMEM_LIMIT)


def _ln(x, g, b):
    mu = jnp.mean(x, axis=-1, keepdims=True)
    xc = x - mu
    var = jnp.mean(xc * xc, axis=-1, keepdims=True)
    return xc * lax.rsqrt(var + LN_EPS) * g + b


def _silu(x):
    return x * jax.nn.sigmoid(x)


def _log_sigmoid(x):
    return jnp.minimum(x, 0.0) - jnp.log1p(jnp.exp(-jnp.abs(x)))


def _dot(a, b):
    return jnp.dot(a, b, preferred_element_type=F32)


def _dot_nt(a, b):
    return lax.dot_general(a, b, (((1,), (1,)), ((), ())), preferred_element_type=F32)


def _dot_tn(a, b):
    return lax.dot_general(a, b, (((0,), (0,)), ((), ())), preferred_element_type=F32)


class _Layer:
    def __init__(self, array, index):
        self.array, self.index = array, index


def _full_spec(a):
    if isinstance(a, _Layer):
        nd, l = a.array.ndim, a.index
        return pl.BlockSpec((1,) + a.array.shape[1:], lambda *_: (l,) + (0,) * (nd - 1))
    nd = a.ndim
    return pl.BlockSpec(a.shape, lambda *_: (0,) * nd)


def _arr(a):
    return a.array if isinstance(a, _Layer) else a


def _rowwise(body, rows, ins, consts, outs, name):
    tm = ROW_TILE
    in_specs = [pl.BlockSpec((tm, w), functools.partial(lambda i, cb: (i, cb), cb=cb)) for _, w, cb in ins]
    in_specs += [_full_spec(c) for c in consts]
    out_specs = [pl.BlockSpec((tm, w), lambda i: (i, 0)) for w, _ in outs]
    out_shape = [jax.ShapeDtypeStruct((rows, w), dt) for w, dt in outs]
    return pl.pallas_call(
        body, out_shape=out_shape, grid=(rows // tm,), in_specs=in_specs, out_specs=out_specs,
        compiler_params=_params(("arbitrary",)), name=name,
    )(*[a for a, _, _ in ins], *[_arr(c) for c in consts])


def _ln_body(x_ref, g_ref, b_ref, o_ref):
    o_ref[...] = _ln(x_ref[...], g_ref[...], b_ref[...])


def _in_proj_body(x_ref, w_ref, o_ref):
    o_ref[...] = _dot(x_ref[...].astype(BF16), w_ref[0]).astype(o_ref.dtype)


def _in_proj(x, w, l):
    rows, d = x.shape
    n = w.shape[2]
    tm, tn = ROW_TILE, min(4096, n)
    return pl.pallas_call(
        _in_proj_body, out_shape=jax.ShapeDtypeStruct((rows, n), BF16), grid=(n // tn, rows // tm),
        in_specs=[pl.BlockSpec((tm, d), lambda j, i: (i, 0)), pl.BlockSpec((1, d, tn), lambda j, i: (l, 0, j))],
        out_specs=pl.BlockSpec((tm, tn), lambda j, i: (i, j)),
        compiler_params=_params(("arbitrary", "arbitrary")), name="in_proj",
    )(x, w)


def _matmul_body(x_ref, w_ref, o_ref):
    o_ref[...] = _dot(x_ref[...].astype(BF16), w_ref[0])


def _qkv_body(cm_ref, xm_ref, wq_ref, wk_ref, wv_ref, wif_ref, wift_ref, bc_ref, br_ref,
              q_ref, k_ref, v_ref, gc_ref, gr_ref, wbd_ref):
    di = cm_ref.shape[1]
    nb = di // MXU_DIM
    qb = wq_ref.shape[0]

    @pl.when(pl.program_id(0) == 0)
    def _():
        c = lax.broadcasted_iota(I32, (MXU_DIM, MXU_DIM), 0)
        r = lax.broadcasted_iota(I32, (MXU_DIM, MXU_DIM), 1)
        same_block = (r // qb) == (c // qb)
        masks = [jnp.logical_and(same_block, (c % qb) == o) for o in range(qb)]
        for s, w_ref in enumerate((wq_ref, wk_ref, wv_ref)):
            for j in range(nb):
                cols = w_ref[:, j * MXU_DIM:(j + 1) * MXU_DIM]
                tile_t = functools.reduce(jnp.add, [jnp.where(masks[o], cols[o:o + 1, :], 0.0) for o in range(qb)])
                wbd_ref[s, j] = tile_t.T.astype(BF16)

    gc = jnp.zeros(gc_ref.shape, F32) + bc_ref[...]
    gr = jnp.zeros(gr_ref.shape, F32) + br_ref[...]
    for j in range(nb):
        sl = slice(j * MXU_DIM, (j + 1) * MXU_DIM)
        cmj = cm_ref[:, sl].astype(BF16)
        xmj = xm_ref[:, sl].astype(BF16)
        for o_ref, src, seg in ((q_ref, cmj, 0), (k_ref, cmj, 1), (v_ref, xmj, 2)):
            y = _dot(src, wbd_ref[seg, j])
            o_ref[:, sl] = y
            yb = y.astype(BF16)
            ws = slice(seg * di + j * MXU_DIM, seg * di + (j + 1) * MXU_DIM)
            gc = gc + _dot(yb, wif_ref[ws, :])
            gr = gr + _dot_nt(wift_ref[:, ws], yb)
    gc_ref[...] = gc
    gr_ref[...] = gr


def _qkv(cm, proj, xm_block, wq, wk, wv, wif, wift, bc, br):
    rows, di = cm.shape
    tm = ROW_TILE
    row_spec = lambda w, cb: pl.BlockSpec((tm, w), functools.partial(lambda i, cb: (i, cb), cb=cb))
    consts = (wq, wk, wv, wif, wift, bc, br)
    return pl.pallas_call(
        _qkv_body,
        out_shape=[jax.ShapeDtypeStruct((rows, di), F32)] * 3
        + [jax.ShapeDtypeStruct((rows, LANES), F32), jax.ShapeDtypeStruct((SUBLANES, rows), F32)],
        grid=(rows // tm,),
        in_specs=[row_spec(di, 0), row_spec(di, xm_block)] + [_full_spec(c) for c in consts],
        out_specs=[row_spec(di, 0)] * 3 + [row_spec(LANES, 0), pl.BlockSpec((SUBLANES, tm), lambda i: (0, i))],
        scratch_shapes=[pltpu.VMEM((3, di // MXU_DIM, MXU_DIM, MXU_DIM), BF16)],
        compiler_params=_params(("arbitrary",)), name="qkv_gates",
    )(cm, proj, *consts)


def _down_body(hh_ref, cm_ref, og_ref, g_ref, skip_ref, w_ref, o_ref):
    hb = (hh_ref[...] * g_ref[...] + skip_ref[...] * cm_ref[...]) * jax.nn.sigmoid(og_ref[...].astype(F32))
    o_ref[...] = _dot(hb.astype(BF16), w_ref[0])


def _merge_body(alpha, ya_ref, yb_ref, ga_ref, gb_ref, x_ref, w_ref, g_ref, b_ref, o_ref):
    merged = (jax.nn.sigmoid(ga_ref[...].astype(F32)) * ya_ref[...]
              + jax.nn.sigmoid(gb_ref[...].astype(F32)) * yb_ref[...])
    o = _dot(merged.astype(BF16), w_ref[0])
    o_ref[...] = _ln(alpha * x_ref[...] + o, g_ref[...], b_ref[...])


def _glu_pre(a, b):
    return a * jax.nn.sigmoid(b)


def _conv_taps(k_taps):
    hp = _round_up(k_taps - 1, SUBLANES)
    return hp, hp - (k_taps - 1)


def _conv_window(s_ref, w_ref, bias, n, k_taps):
    _, off = _conv_taps(k_taps)
    acc = jnp.zeros((n, s_ref.shape[1]), F32) + bias
    for k in range(k_taps):
        acc = acc + w_ref[k:k + 1, :] * s_ref[k + off:k + off + n, :]
    return acc


def _conv_long(s_ref, sh_ref, w_ref, bias, out_ref, post, n, k_taps):
    hp, off = _conv_taps(k_taps)
    c = s_ref.shape[1]
    shifts = sorted({(k + off) % SUBLANES for k in range(k_taps)} - {0})
    span = n + hp - SUBLANES
    for s in shifts:
        sh_ref[s, 0:span, :] = s_ref[s:s + span, :]
    rg = SUBLANES * max(1, min(4, 32 * 1024 // (SUBLANES * c)))
    while n % rg:
        rg //= 2

    def body(i, carry):
        t0 = pl.multiple_of(i * rg, rg)
        acc = jnp.zeros((rg, c), F32) + bias
        for k in range(k_taps):
            g, s = divmod(k + off, SUBLANES)
            if s == 0:
                x = s_ref[pl.ds(t0 + SUBLANES * g, rg), :]
            else:
                x = sh_ref[s, pl.ds(t0 + SUBLANES * g, rg), :]
            acc = acc + w_ref[k:k + 1, :] * x
        out_ref[pl.ds(t0, rg), :] = post(acc)
        return carry

    lax.fori_loop(0, n // rg, body, 0)


def _conv_prompt_body(n_pre, k_taps, post, bsz, *refs):
    main = refs[:n_pre]
    meta = refs[n_pre:2 * n_pre]
    w_ref, b_ref = refs[2 * n_pre:2 * n_pre + 2]
    extra = refs[2 * n_pre + 2:-5]
    out_main, out_meta, state_ref, s_ref, sh_ref = refs[-5:]
    pre = _glu_pre if n_pre == 2 else (lambda a: a)
    post_fn = functools.partial(post, *[e[...] for e in extra]) if extra else post
    hp, off = _conv_taps(k_taps)
    n_meta, tl = out_meta.shape[0], out_main.shape[0]
    c = pl.program_id(1)
    real = pl.program_id(0) < bsz
    bias = b_ref[...]

    @pl.when(jnp.logical_and(real, c == 0))
    def _():
        s_ref[0:hp, :] = jnp.zeros((hp, s_ref.shape[1]), F32)
        s_ref[hp:hp + n_meta, :] = pre(*[r[...].astype(F32) for r in meta])
        out_meta[...] = post_fn(_conv_window(s_ref, w_ref, bias, n_meta, k_taps))
        s_ref[0:hp, :] = s_ref[n_meta:n_meta + hp, :]

    @pl.when(jnp.logical_and(real, c > 0))
    def _():
        s_ref[hp:hp + tl, :] = pre(*[r[...].astype(F32) for r in main])
        _conv_long(s_ref, sh_ref, w_ref, bias, out_main, post_fn, tl, k_taps)
        s_ref[0:hp, :] = s_ref[tl:tl + hp, :]

    @pl.when(jnp.logical_and(real, c == pl.num_programs(1) - 1))
    def _():
        state_ref[0] = s_ref[off:hp, :]

    @pl.when(jnp.logical_and(jnp.logical_not(real), c > 0))
    def _():
        out_main[...] = jnp.zeros(out_main.shape, F32)


def _prompt_rows(bsz, seq, t_pad):
    tl = min(SEQ_TILE, seq)
    nc = seq // tl
    blocks = t_pad // tl
    extra = -(-(blocks - bsz * nc) // nc)
    block = lambda bi, ci: jnp.minimum(bi * nc + jnp.maximum(ci - 1, 0), blocks - 1)
    return tl, nc, extra, block


def _conv_prompt(proj, col_blocks, w, b, extra, post, dims, name):
    bsz, seq, n_meta, t_pad = dims
    k_taps, c = w.shape
    hp, _ = _conv_taps(k_taps)
    tl, nc, n_extra, block = _prompt_rows(bsz, seq, t_pad)
    meta_blk0 = bsz * seq // n_meta
    n_pre = len(col_blocks)
    seq_i = lambda bi: jnp.minimum(bi, bsz - 1)
    main_map = lambda cb: (lambda bi, ci: (block(bi, ci), cb))
    meta_map = lambda cb: (lambda bi, ci: (meta_blk0 + seq_i(bi), cb))
    in_specs = [pl.BlockSpec((tl, c), main_map(cb)) for cb in col_blocks]
    in_specs += [pl.BlockSpec((n_meta, c), meta_map(cb)) for cb in col_blocks]
    consts = (w, b) + tuple(extra)
    in_specs += [_full_spec(a) for a in consts]
    return pl.pallas_call(
        functools.partial(_conv_prompt_body, n_pre, k_taps, post, bsz),
        out_shape=[jax.ShapeDtypeStruct((t_pad, c), F32), jax.ShapeDtypeStruct((bsz * n_meta, c), F32),
                   jax.ShapeDtypeStruct((bsz, k_taps - 1, c), F32)],
        grid=(bsz + n_extra, 1 + nc), in_specs=in_specs,
        out_specs=[pl.BlockSpec((tl, c), main_map(0)), pl.BlockSpec((n_meta, c), lambda bi, ci: (seq_i(bi), 0)),
                   pl.BlockSpec((1, k_taps - 1, c), lambda bi, ci: (seq_i(bi), 0, 0))],
        scratch_shapes=[pltpu.VMEM((hp + tl, c), F32), pltpu.VMEM((SUBLANES, hp + tl, c), F32)],
        compiler_params=_params(("arbitrary", "arbitrary")), name=name,
    )(*([proj] * (2 * n_pre)), *consts)


def _conv_sample_body(n_pre, k_taps, post, seqs, *refs):
    rows = refs[:n_pre]
    hist_ref, w_ref, b_ref = refs[n_pre:n_pre + 3]
    extra = refs[n_pre + 3:-4]
    out_ref, state_ref, s_ref, u_ref = refs[-4:]
    pre = _glu_pre if n_pre == 2 else (lambda a: a)
    post_fn = functools.partial(post, *[e[...] for e in extra]) if extra else post
    hp, off = _conv_taps(k_taps)
    n = out_ref.shape[0] // seqs
    bias = b_ref[...]
    u_ref[...] = pre(*[r[...].astype(F32) for r in rows])

    def body(j, carry):
        r0 = pl.multiple_of(j * n, n)
        s_ref[off:hp, :] = hist_ref[j]
        s_ref[hp:hp + n, :] = u_ref[pl.ds(r0, n), :]
        out_ref[pl.ds(r0, n), :] = post_fn(_conv_window(s_ref, w_ref, bias, n, k_taps))
        state_ref[j] = s_ref[off + n:hp + n, :]
        return carry

    lax.fori_loop(0, seqs, body, 0)


def _conv_sample(proj, col_blocks, hist, layer, w, b, extra, post, dims, name):
    row0, dbsz, n = dims
    k_taps, c = w.shape
    hp, _ = _conv_taps(k_taps)
    seqs = min(SUBLANES, dbsz)
    blk0 = row0 // (seqs * n)
    hist0 = layer * (dbsz // seqs)
    n_pre = len(col_blocks)
    consts = (w, b) + tuple(extra)
    in_specs = [pl.BlockSpec((seqs * n, c), functools.partial(lambda i, cb: (blk0 + i, cb), cb=cb)) for cb in col_blocks]
    in_specs += [pl.BlockSpec((seqs, k_taps - 1, c), lambda i: (hist0 + i, 0, 0))]
    in_specs += [_full_spec(a) for a in consts]
    return pl.pallas_call(
        functools.partial(_conv_sample_body, n_pre, k_taps, post, seqs),
        out_shape=[jax.ShapeDtypeStruct((dbsz * n, c), F32), jax.ShapeDtypeStruct((dbsz, k_taps - 1, c), F32)],
        grid=(dbsz // seqs,), in_specs=in_specs,
        out_specs=[pl.BlockSpec((seqs * n, c), lambda i: (i, 0)), pl.BlockSpec((seqs, k_taps - 1, c), lambda i: (i, 0, 0))],
        scratch_shapes=[pltpu.VMEM((hp + n, c), F32), pltpu.VMEM((seqs * n, c), F32)],
        compiler_params=_params(("arbitrary",)), name=name,
    )(*([proj] * n_pre), hist, *consts)


def _post_ln_silu(g, b, acc):
    return _silu(_ln(acc, g, b))


def _mlstm_chunk(gc, gr, q, k, v, c0, n0, m0, hsel, n_heads):
    seq, hd = q.shape
    lane = lax.broadcasted_iota(I32, gc.shape, 1)
    sub = lax.broadcasted_iota(I32, gr.shape, 0)
    pick_c = lambda j: jnp.sum(jnp.where(lane == j, gc, 0.0), axis=1, keepdims=True)
    pick_r = lambda j: jnp.sum(jnp.where(sub == j, gr, 0.0), axis=0, keepdims=True)
    ig_c, lf_c = pick_c(hsel), _log_sigmoid(pick_c(hsel + n_heads))
    ig_r, lf_r = pick_r(hsel), _log_sigmoid(pick_r(hsel + n_heads))
    ti = lax.broadcasted_iota(I32, (seq, seq), 0)
    si = lax.broadcasted_iota(I32, (seq, seq), 1)
    causal = si <= ti
    b_c = jnp.sum(jnp.where(causal, lf_r, 0.0), axis=1, keepdims=True)
    b_r = jnp.sum(jnp.where(ti <= si, lf_c, 0.0), axis=0, keepdims=True)
    log_w = jnp.where(causal, b_c - b_r + ig_r, -jnp.inf)
    log_s = b_c + m0
    m_c = jnp.maximum(log_s, jnp.max(log_w, axis=1, keepdims=True))
    w = jnp.exp(log_w - m_c)
    s_c = jnp.exp(log_s - m_c)
    ks = k * (hd ** -0.5)
    qb, kb, vb = q.astype(BF16), ks.astype(BF16), v.astype(BF16)
    qk = _dot_nt(qb, kb) * w
    num = s_c * _dot(qb, c0.astype(BF16)) + _dot(qk.astype(BF16), vb)
    den = s_c * jnp.sum(q * n0, axis=1, keepdims=True) + jnp.sum(qk, axis=1, keepdims=True)
    hval = num / jnp.maximum(jnp.abs(den), jnp.exp(-m_c))
    mu = jnp.mean(hval, axis=1, keepdims=True)
    hc = hval - mu
    var = jnp.mean(hc * hc, axis=1, keepdims=True)
    hh = hc * lax.rsqrt(var + LN_EPS)
    m_end = m_c[seq - 1:seq, :]
    b_last = b_c[seq - 1:seq, :]
    w_end = jnp.exp(b_last - b_c + ig_c - m_end)
    s_end = jnp.exp(b_last + m0 - m_end)
    kw = ks * w_end
    c1 = s_end * c0 + _dot_tn(kw.astype(BF16), vb)
    n1 = s_end * n0 + jnp.sum(kw, axis=0, keepdims=True)
    return hh, c1, n1, m_end


def _mlstm_prompt_body(n_heads, bsz, gcm_ref, grm_ref, qm_ref, km_ref, vm_ref, gc_ref, gr_ref, q_ref, k_ref, v_ref,
                       hm_ref, h_ref, c_out, n_out, m_out, c_s, n_s, m_s):
    c = pl.program_id(1)
    real = pl.program_id(0) < bsz
    hd = q_ref.shape[1] // n_heads

    def step(gc, gr, q, k, v, out):
        for h in range(n_heads):
            cols = slice(h * hd, (h + 1) * hd)
            hh, c1, n1, m1 = _mlstm_chunk(gc, gr, q[:, cols], k[:, cols], v[:, cols],
                                          c_s[h], n_s[h:h + 1, :], m_s[h:h + 1, 0:1], h, n_heads)
            out[:, cols] = hh
            c_s[h] = c1
            n_s[h:h + 1, :] = n1
            m_s[h:h + 1, :] = jnp.broadcast_to(m1, (1, m_s.shape[1]))

    @pl.when(jnp.logical_and(real, c == 0))
    def _():
        c_s[...] = jnp.zeros(c_s.shape, F32)
        n_s[...] = jnp.zeros(n_s.shape, F32)
        m_s[...] = jnp.zeros(m_s.shape, F32)
        step(gcm_ref[...], grm_ref[0], qm_ref, km_ref, vm_ref, hm_ref)

    @pl.when(jnp.logical_and(real, c > 0))
    def _():
        step(gc_ref[...], gr_ref[...], q_ref, k_ref, v_ref, h_ref)

    @pl.when(jnp.logical_and(real, c == pl.num_programs(1) - 1))
    def _():
        c_out[0] = c_s[...]
        n_out[0] = n_s[...]
        m_out[0] = m_s[...]

    @pl.when(jnp.logical_and(jnp.logical_not(real), c > 0))
    def _():
        h_ref[...] = jnp.zeros(h_ref.shape, F32)


def _mlstm_prompt(q, k, v, gcol, grow, grow_meta, dims, n_heads):
    bsz, seq, n_meta, t_pad = dims
    di = q.shape[1]
    hd = di // n_heads
    tl, nc, n_extra, block = _prompt_rows(bsz, seq, t_pad)
    meta_blk0 = bsz * seq // n_meta
    seq_i = lambda bi: jnp.minimum(bi, bsz - 1)
    qkv_main = pl.BlockSpec((tl, di), lambda bi, ci: (block(bi, ci), 0))
    qkv_meta = pl.BlockSpec((n_meta, di), lambda bi, ci: (meta_blk0 + seq_i(bi), 0))
    st = lambda shape: pl.BlockSpec((1,) + shape, lambda bi, ci: (seq_i(bi),) + (0,) * len(shape))
    return pl.pallas_call(
        functools.partial(_mlstm_prompt_body, n_heads, bsz),
        out_shape=[jax.ShapeDtypeStruct((bsz * n_meta, di), F32), jax.ShapeDtypeStruct((t_pad, di), F32),
                   jax.ShapeDtypeStruct((bsz, n_heads, hd, hd), F32), jax.ShapeDtypeStruct((bsz, n_heads, hd), F32),
                   jax.ShapeDtypeStruct((bsz, n_heads, LANES), F32)],
        grid=(bsz + n_extra, 1 + nc),
        in_specs=[pl.BlockSpec((n_meta, LANES), lambda bi, ci: (meta_blk0 + seq_i(bi), 0)),
                  pl.BlockSpec((1, SUBLANES, n_meta), lambda bi, ci: (seq_i(bi), 0, 0)),
                  qkv_meta, qkv_meta, qkv_meta,
                  pl.BlockSpec((tl, LANES), lambda bi, ci: (block(bi, ci), 0)),
                  pl.BlockSpec((SUBLANES, tl), lambda bi, ci: (0, block(bi, ci))),
                  qkv_main, qkv_main, qkv_main],
        out_specs=[pl.BlockSpec((n_meta, di), lambda bi, ci: (seq_i(bi), 0)), qkv_main,
                   st((n_heads, hd, hd)), st((n_heads, hd)), st((n_heads, LANES))],
        scratch_shapes=[pltpu.VMEM((n_heads, hd, hd), F32), pltpu.VMEM((n_heads, hd), F32),
                        pltpu.VMEM((n_heads, LANES), F32)],
        compiler_params=_params(("arbitrary",) * 2), name="mlstm_prompt",
    )(gcol, grow_meta, q, k, v, gcol, grow, q, k, v)


def _mlstm_sample_body(n_heads, has_acc, gc_ref, gr_ref, q_ref, k_ref, v_ref, c0_ref, n0_ref, m0_ref, *refs):
    h_ref, c_out, n_out, m_out = refs[1:] if has_acc else refs
    hd = q_ref.shape[1] // n_heads
    gc, gr = gc_ref[...], gr_ref[0]
    for h in range(n_heads):
        cols = slice(h * hd, (h + 1) * hd)
        hh, c1, n1, m1 = _mlstm_chunk(gc, gr, q_ref[:, cols], k_ref[:, cols], v_ref[:, cols],
                                      c0_ref[0, h], n0_ref[0, h:h + 1, :], m0_ref[0, h:h + 1, 0:1], h, n_heads)
        h_ref[:, cols] = hh
        c_out[0, h] = c1
        n_out[0, h:h + 1, :] = n1
        m_out[0, h:h + 1, :] = jnp.broadcast_to(m1, (1, m_out.shape[2]))


def _mlstm_sample(q, k, v, gcol, grow_s, c0_all, c_acc, layer, n0, m0, dims, n_heads):
    row0, dbsz, n = dims
    di = q.shape[1]
    hd = di // n_heads
    blk0 = row0 // n
    st0 = layer * dbsz
    qkv = pl.BlockSpec((n, di), lambda bi: (blk0 + bi, 0))
    c_spec = pl.BlockSpec((1, n_heads, hd, hd), lambda bi: (st0 + bi, 0, 0, 0))
    small = lambda w: pl.BlockSpec((1, n_heads, w), lambda bi: (bi, 0, 0))
    has_acc = c_acc is not None
    operands = (gcol, grow_s, q, k, v, c0_all, n0, m0) + ((c_acc,) if has_acc else ())
    return pl.pallas_call(
        functools.partial(_mlstm_sample_body, n_heads, has_acc),
        out_shape=[jax.ShapeDtypeStruct((dbsz * n, di), F32), jax.ShapeDtypeStruct(c0_all.shape, F32),
                   jax.ShapeDtypeStruct((dbsz, n_heads, hd), F32), jax.ShapeDtypeStruct((dbsz, n_heads, LANES), F32)],
        grid=(dbsz,),
        in_specs=[pl.BlockSpec((n, LANES), lambda bi: (blk0 + bi, 0)),
                  pl.BlockSpec((1, SUBLANES, n), lambda bi: (bi, 0, 0)),
                  qkv, qkv, qkv, c_spec, small(hd), small(LANES)]
        + ([pl.BlockSpec(memory_space=pl.ANY)] if has_acc else []),
        out_specs=[pl.BlockSpec((n, di), lambda bi: (bi, 0)), c_spec, small(hd), small(LANES)],
        input_output_aliases={8: 1} if has_acc else {},
        compiler_params=_params(("arbitrary",)), name="mlstm_sample",
    )(*operands)


def _stack_rows(rows, n):
    sub = lax.broadcasted_iota(I32, (len(rows), n), 0)
    out = jnp.zeros((len(rows), n), rows[0].dtype)
    for j, r in enumerate(rows):
        out = jnp.where(sub == j, r, out)
    return out


def _router_body(h_ref, wt_ref, bias_ref, te_ref, wk_ref):
    n_exp = wt_ref.shape[0]
    gsz = n_exp // N_GROUPS
    tm = h_ref.shape[0]
    logits = _dot_nt(wt_ref[...], h_ref[...].astype(BF16))
    scores = jax.nn.sigmoid(logits)
    choice = scores + bias_ref[...]
    sub = lax.broadcasted_iota(I32, (gsz, tm), 0)
    neg = -jnp.inf

    def first_max(x):
        m = jnp.max(x, axis=0, keepdims=True)
        idx = jnp.min(jnp.where(x == m, sub, gsz), axis=0, keepdims=True)
        return m, idx

    sc = [scores[g * gsz:(g + 1) * gsz, :] for g in range(N_GROUPS)]
    ch = [choice[g * gsz:(g + 1) * gsz, :] for g in range(N_GROUPS)]
    gs = []
    for x in ch:
        m1, i1 = first_max(x)
        m2, _ = first_max(jnp.where(sub == i1, neg, x))
        gs.append(m1 + m2)
    y = _stack_rows(gs, tm)
    gsub = lax.broadcasted_iota(I32, (N_GROUPS, tm), 0)
    gsel = jnp.zeros((N_GROUPS, tm), jnp.bool_)
    for _ in range(TOPK_GROUPS):
        m = jnp.max(y, axis=0, keepdims=True)
        idx = jnp.min(jnp.where(y == m, gsub, N_GROUPS), axis=0, keepdims=True)
        hit = gsub == idx
        gsel = jnp.logical_or(gsel, hit)
        y = jnp.where(hit, neg, y)
    gself = gsel.astype(F32)
    ch = [jnp.where(gself[g:g + 1, :] > 0.0, ch[g], neg) for g in range(N_GROUPS)]
    eid = [sub + g * gsz for g in range(N_GROUPS)]
    picks, pick_w = [], []
    for _ in range(TOP_K):
        m = functools.reduce(jnp.maximum, [jnp.max(x, axis=0, keepdims=True) for x in ch])
        idx = functools.reduce(jnp.minimum, [jnp.min(jnp.where(x == m, e, n_exp), axis=0, keepdims=True)
                                             for x, e in zip(ch, eid)])
        hits = [e == idx for e in eid]
        pick_w.append(functools.reduce(jnp.add, [jnp.sum(jnp.where(h, s, 0.0), axis=0, keepdims=True)
                                                 for h, s in zip(hits, sc)]))
        ch = [jnp.where(h, neg, x) for h, x in zip(hits, ch)]
        picks.append(idx)
    total = functools.reduce(jnp.add, pick_w)
    te_ref[...] = _stack_rows(picks, tm)
    wk_ref[...] = _stack_rows([w / total * ROUTE_SCALE for w in pick_w], tm)


def _router(h, wt, bias):
    rows, d = h.shape
    tm = ROW_TILE
    return pl.pallas_call(
        _router_body,
        out_shape=[jax.ShapeDtypeStruct((TOP_K, rows), I32), jax.ShapeDtypeStruct((TOP_K, rows), F32)],
        grid=(rows // tm,),
        in_specs=[pl.BlockSpec((tm, d), lambda i: (i, 0)), _full_spec(wt), _full_spec(bias)],
        out_specs=[pl.BlockSpec((TOP_K, tm), lambda i: (0, i))] * 2,
        compiler_params=_params(("arbitrary",)), name="router",
    )(h, wt, bias)


def _dispatch_body(n_blocks, te_ref, dest_ref, blke_ref, used_ref, start_ref, cnt_ref, cnt_s, start_s, carry_s):
    n_exp = cnt_s.shape[0]
    tw = te_ref.shape[1]
    phase, i = pl.program_id(0), pl.program_id(1)
    te = te_ref[...]
    eio = lax.broadcasted_iota(I32, (n_exp, tw), 0)
    sel = functools.reduce(jnp.logical_or, [eio == te[k:k + 1, :] for k in range(TOP_K)])
    self = sel.astype(F32)

    @pl.when(jnp.logical_and(phase == 0, i == 0))
    def _():
        cnt_s[...] = jnp.zeros(cnt_s.shape, F32)

    @pl.when(phase == 0)
    def _():
        cnt_s[...] += jnp.sum(self, axis=1, keepdims=True)

    @pl.when(jnp.logical_and(phase == 1, i == 0))
    def _():
        cnt = cnt_s[...]
        nblk = jnp.floor((cnt + (EXPERT_ROWS - 1)) * (1.0 / EXPERT_ROWS))
        r = lax.broadcasted_iota(I32, (n_exp, n_exp), 0)
        c = lax.broadcasted_iota(I32, (n_exp, n_exp), 1)
        lower = (c < r).astype(BF16)
        blk_start = _dot(lower, nblk.astype(BF16))
        blk_end = blk_start + nblk
        start_s[...] = blk_start * EXPERT_ROWS
        carry_s[...] = jnp.zeros(carry_s.shape, F32)
        start_ref[...] = (blk_start * EXPERT_ROWS).astype(I32)
        cnt_ref[...] = cnt.astype(I32)
        used_ref[...] = jnp.sum(nblk, axis=0, keepdims=True).astype(I32)
        jio = lax.broadcasted_iota(I32, (n_exp, blke_ref.shape[1]), 1).astype(F32)
        be = jnp.sum((blk_end[:, 0:1] <= jio).astype(F32), axis=0, keepdims=True)
        blke_ref[...] = jnp.minimum(be, n_exp - 1.0).astype(I32)

    @pl.when(phase == 1)
    def _():
        r = lax.broadcasted_iota(I32, (tw, tw), 0)
        c = lax.broadcasted_iota(I32, (tw, tw), 1)
        upper = (r <= c).astype(BF16)
        incl = _dot(self.astype(BF16), upper)
        dest = start_s[:, 0:1] + carry_s[:, 0:1] + incl - self
        carry_s[...] += jnp.sum(self, axis=1, keepdims=True)
        rows = [jnp.sum(jnp.where(eio == te[k:k + 1, :], dest, 0.0), axis=0, keepdims=True) for k in range(TOP_K)]
        dest_ref[...] = _stack_rows(rows, tw).astype(I32)


def _dispatch(te, n_exp, n_blocks):
    rows = te.shape[1]
    tw = ROW_TILE
    nbp = _round_up(n_blocks, LANES)
    small = lambda shape: pl.BlockSpec(shape, lambda p, i: (0, 0))
    return pl.pallas_call(
        functools.partial(_dispatch_body, n_blocks),
        out_shape=[jax.ShapeDtypeStruct((TOP_K, rows), I32), jax.ShapeDtypeStruct((1, nbp), I32),
                   jax.ShapeDtypeStruct((1, LANES), I32), jax.ShapeDtypeStruct((n_exp, LANES), I32),
                   jax.ShapeDtypeStruct((n_exp, LANES), I32)],
        grid=(2, rows // tw),
        in_specs=[pl.BlockSpec((TOP_K, tw), lambda p, i: (0, i))],
        out_specs=[pl.BlockSpec((TOP_K, tw), lambda p, i: (0, i * p)), small((1, nbp)), small((1, LANES)),
                   small((n_exp, LANES)), small((n_exp, LANES))],
        scratch_shapes=[pltpu.VMEM((n_exp, LANES), F32)] * 3,
        compiler_params=_params(("arbitrary", "arbitrary")), name="dispatch",
    )(te)


def _to_tiles(ref, x, rows=slice(None)):
    ref[rows] = x.reshape((x.shape[0],) + ref.shape[-2:])


def _from_tiles(ref, lead=(), rows=slice(None)):
    x = ref[lead + (rows,)]
    return x.reshape(x.shape[0], x.shape[1] * x.shape[2])


def _scatter_body(start_ref, cnt_ref, used_ref, dest_ref, h_ref, xs_ref, h3_ref, zero_ref, sem):
    tm = h_ref.shape[0]
    _to_tiles(h3_ref, h_ref[...])
    row_copy = lambda t, r: pltpu.make_async_copy(h3_ref.at[t], xs_ref.at[r], sem)

    def issue(t, carry):
        for k in range(TOP_K):
            row_copy(t, dest_ref[k, t]).start(priority=k % 2)
        return carry

    lax.fori_loop(0, tm, issue, 0)
    for k in range(TOP_K):
        pltpu.make_async_copy(h3_ref, xs_ref.at[pl.ds(0, tm)], sem).wait()

    @pl.when(pl.program_id(0) == pl.num_programs(0) - 1)
    def _():
        zero_ref[...] = jnp.zeros(zero_ref.shape, F32)
        sizes = [1 << b for b in range(EXPERT_ROWS.bit_length() - 1)]

        def fill(e, carry):
            cnt = cnt_ref[e]
            pad = (-cnt) & (EXPERT_ROWS - 1)
            row = start_ref[e] + cnt
            for sz in sizes:
                cp = pltpu.make_async_copy(zero_ref.at[pl.ds(0, sz)], xs_ref.at[pl.ds(row, sz)], sem)

                @pl.when((pad & sz) != 0)
                def _():
                    cp.start()
                    cp.wait()

                row = row + (pad & sz)
            return carry

        lax.fori_loop(0, start_ref.shape[0], fill, 0)

        def fill_block(j, carry):
            for r0 in range(0, EXPERT_ROWS, zero_ref.shape[0]):
                cp = pltpu.make_async_copy(zero_ref, xs_ref.at[pl.ds(j * EXPERT_ROWS + r0, zero_ref.shape[0])], sem)
                cp.start()
                cp.wait()
            return carry

        lax.fori_loop(used_ref[0], xs_ref.shape[0] // EXPERT_ROWS, fill_block, 0)


def _scatter(h, dest, start, cnt, used, n_blocks):
    rows, d = h.shape
    tm = ROW_TILE // 2
    lt = d // LANES
    grid_spec = pltpu.PrefetchScalarGridSpec(
        num_scalar_prefetch=3, grid=(rows // tm,),
        in_specs=[pl.BlockSpec((TOP_K, tm), lambda i, *_: (0, i), memory_space=pltpu.SMEM),
                  pl.BlockSpec((tm, d), lambda i, *_: (i, 0))],
        out_specs=pl.BlockSpec(memory_space=pl.ANY),
        scratch_shapes=[pltpu.VMEM((tm, lt, LANES), F32), pltpu.VMEM((EXPERT_ROWS // 2, lt, LANES), F32),
                        pltpu.SemaphoreType.DMA],
    )
    return pl.pallas_call(
        _scatter_body, out_shape=jax.ShapeDtypeStruct((n_blocks * EXPERT_ROWS, lt, LANES), F32), grid_spec=grid_spec,
        compiler_params=_params(("arbitrary",)), name="moe_scatter",
    )(start, cnt, used, dest, h)


def _expert_body(blke_ref, used_ref, xs_ref, w1_ref, w3_ref, w2_ref, ys_ref, w1_s, w3_s, w2_s):
    j = pl.program_id(0)
    prev = blke_ref[jnp.maximum(j - 1, 0)]
    live = j < used_ref[0]

    @pl.when(jnp.logical_and(live, jnp.logical_or(j == 0, blke_ref[j] != prev)))
    def _():
        w1_s[...] = w1_ref[0].astype(BF16)
        w3_s[...] = w3_ref[0].astype(BF16)
        w2_s[...] = w2_ref[0].astype(BF16)

    @pl.when(live)
    def _():
        half = EXPERT_ROWS // 2
        for r0 in (0, half):
            rows = slice(r0, r0 + half)
            x = _from_tiles(xs_ref, rows=rows).astype(BF16)
            mid = _silu(_dot(x, w1_s[...])) * _dot(x, w3_s[...])
            _to_tiles(ys_ref, _dot(mid.astype(BF16), w2_s[...]), rows)

    @pl.when(jnp.logical_not(live))
    def _():
        ys_ref[...] = jnp.zeros(ys_ref.shape, F32)


def _experts(xs, blke, used, w1, w3, w2, e0):
    rows, lt, _ = xs.shape
    d, de = w1.shape[1:]
    n_blocks = rows // EXPERT_ROWS
    blk = lambda j, blke_ref, used_ref: jnp.minimum(j, used_ref[0] - 1)
    xmap = lambda j, b, u: (blk(j, b, u), 0, 0)
    wmap = lambda j, b, u: (e0 + b[blk(j, b, u)], 0, 0)
    grid_spec = pltpu.PrefetchScalarGridSpec(
        num_scalar_prefetch=2, grid=(n_blocks,),
        in_specs=[pl.BlockSpec((EXPERT_ROWS, lt, LANES), xmap), pl.BlockSpec((1, d, de), wmap),
                  pl.BlockSpec((1, d, de), wmap), pl.BlockSpec((1, de, d), wmap)],
        out_specs=pl.BlockSpec((EXPERT_ROWS, lt, LANES), lambda j, b, u: (j, 0, 0)),
        scratch_shapes=[pltpu.VMEM((d, de), BF16), pltpu.VMEM((d, de), BF16), pltpu.VMEM((de, d), BF16)],
    )
    return pl.pallas_call(
        _expert_body, out_shape=jax.ShapeDtypeStruct(xs.shape, F32), grid_spec=grid_spec,
        compiler_params=_params(("arbitrary",)), name="moe_experts",
    )(blke, used, xs, w1, w3, w2)


def _combine_body(alpha, dest_ref, wk_ref, h_ref, ys_ref, ws1_ref, ws3_ref, ws2_ref, g_ref, b_ref, o_ref, buf, sem):
    tm = h_ref.shape[0]
    row_copy = lambda k, t, r: pltpu.make_async_copy(ys_ref.at[r], buf.at[k, t], sem)

    def issue(t, carry):
        for k in range(TOP_K):
            row_copy(k, t, dest_ref[k, t]).start(priority=k % 2)
        return carry

    lax.fori_loop(0, tm, issue, 0)
    h = h_ref[...]
    hb = h.astype(BF16)
    f = _dot((_silu(_dot(hb, ws1_ref[...])) * _dot(hb, ws3_ref[...])).astype(BF16), ws2_ref[...])
    for k in range(TOP_K):
        pltpu.make_async_copy(ys_ref.at[pl.ds(0, tm)], buf.at[k], sem).wait()
    for k in range(TOP_K):
        f = f + wk_ref[:, k:k + 1] * _from_tiles(buf, (k,))
    o_ref[...] = _ln(alpha * h + f, g_ref[...], b_ref[...])


def _combine(alpha, h, ys, dest, wk_col, ws1, ws3, ws2, g, b):
    rows, d = h.shape
    tm = ROW_TILE // 4
    consts = (ws1, ws3, ws2, g, b)
    return pl.pallas_call(
        functools.partial(_combine_body, alpha), out_shape=jax.ShapeDtypeStruct((rows, d), F32), grid=(rows // tm,),
        in_specs=[pl.BlockSpec((TOP_K, tm), lambda i: (0, i), memory_space=pltpu.SMEM),
                  pl.BlockSpec((tm, TOP_K), lambda i: (i, 0)), pl.BlockSpec((tm, d), lambda i: (i, 0)),
                  pl.BlockSpec(memory_space=pl.ANY)] + [_full_spec(c) for c in consts],
        out_specs=pl.BlockSpec((tm, d), lambda i: (i, 0)),
        scratch_shapes=[pltpu.VMEM((TOP_K, tm, d // LANES, LANES), F32), pltpu.SemaphoreType.DMA],
        compiler_params=_params(("arbitrary",)), name="moe_combine",
    )(dest, wk_col, h, ys, *consts)


def _moe(alpha, h, wrt, rbias, w1, w3, w2, layer, ws1, ws3, ws2, g, b):
    rows = h.shape[0]
    n_exp = wrt.shape[0]
    n_blocks = -(-rows * TOP_K // EXPERT_ROWS) + n_exp
    te, wk = _router(h, wrt, rbias)
    dest, blke, used, start, cnt = _dispatch(te, n_exp, n_blocks)
    xs = _scatter(h, dest, start[:, 0], cnt[:, 0], used[0, :1], n_blocks)
    ys = _experts(xs, blke[0, :n_blocks], used[0, :1], w1, w3, w2, layer * n_exp)
    return _combine(alpha, h, ys, dest, wk.T, ws1, ws3, ws2, g, b)


def kernel(x_prompt, x_sample, state_conv, state_mconv, state_C, state_n, state_m, meta_tokens, ln_in_g, ln_in_b, w_in, conv_w, conv_b, conv_ln_g, conv_ln_b, w_conv_out, mconv_w, mconv_b, w_q, w_k, w_v, w_if, b_if, mh_ln_g, skip, w_m_down, w_o, ln1_g, ln1_b, ln2_g, ln2_b, w_router, router_bias, w_e1, w_e3, w_e2, w_s1, w_s3, w_s2):
    bsz, seq, d = x_prompt.shape
    dbsz, dseq, _ = x_sample.shape
    depth = w_in.shape[0]
    n_meta = meta_tokens.shape[0]
    dc = conv_w.shape[2]
    di = mconv_w.shape[2]
    n_heads = state_m.shape[2]
    hd = di // n_heads
    alpha = (2 * depth) ** 0.25
    m0 = bsz * seq
    s0 = m0 + bsz * n_meta
    t_real = s0 + dbsz * dseq
    t_pad = _round_up(t_real, ROW_TILE)
    assert dc == d and di % dc == 0 and m0 % n_meta == 0 and s0 % (SUBLANES * dseq) == 0
    assert seq % min(SEQ_TILE, seq) == 0 and dbsz % SUBLANES == 0 and 2 * n_heads <= SUBLANES
    row = lambda a: a.reshape(1, -1)

    def flat(main, meta, samp):
        out = lax.dynamic_update_slice(main, meta, (m0, 0))
        return lax.dynamic_update_slice(out, samp, (s0, 0))

    x_all = jnp.concatenate([x_prompt.reshape(m0, d), jnp.tile(meta_tokens, (bsz, 1)), x_sample.reshape(dbsz * dseq, d),
                             jnp.zeros((t_pad - t_real, d), F32)], axis=0)
    (x,) = _rowwise(_ln_body, t_pad, [(x_all, d, 0)], [row(ln_in_g), row(ln_in_b)], [(d, F32)], "ln_in")

    xm_blk = 2 * dc // di
    og_blk = (2 * dc + di) // di
    ga_blk = (2 * dc + 2 * di) // d
    outs = {k: [] for k in ("conv_p", "mconv_p", "c_p", "n_p", "m_p", "conv_s", "mconv_s", "n_s", "m_s")}
    pdims, sdims = (bsz, seq, n_meta, t_pad), (s0, dbsz, dseq)
    w_in_b, w_co_b, w_md_b, w_o_b = (w.astype(BF16) for w in (w_in, w_conv_out, w_m_down, w_o))
    n_exp = w_e1.shape[1]
    we1, we3, we2 = (w.reshape((depth * n_exp,) + w.shape[2:]) for w in (w_e1, w_e3, w_e2))
    hist_c = state_conv.reshape((depth * dbsz,) + state_conv.shape[2:])
    hist_m = state_mconv.reshape((depth * dbsz,) + state_mconv.shape[2:])
    c0_all = state_C.reshape(depth * dbsz, n_heads, hd, hd)
    c_s = None
    for l in range(depth):
        proj = _in_proj(x, w_in_b, l)
        cw = (conv_w[l], row(conv_b[l]))
        cl = (row(conv_ln_g[l]), row(conv_ln_b[l]))
        a_main, a_meta, conv_p = _conv_prompt(proj, (0, 1), *cw, cl, _post_ln_silu, pdims, "conv_prompt")
        a_samp, conv_s = _conv_sample(proj, (0, 1), hist_c, l, *cw, cl, _post_ln_silu, sdims, "conv_sample")
        (ya,) = _rowwise(_matmul_body, t_pad, [(flat(a_main, a_meta, a_samp), dc, 0)],
                         [_Layer(w_co_b, l)], [(d, F32)], "conv_out")
        mw = (mconv_w[l], row(mconv_b[l]))
        cm_main, cm_meta, mconv_p = _conv_prompt(proj, (xm_blk,), *mw, (), _silu, pdims, "mconv_prompt")
        cm_samp, mconv_s = _conv_sample(proj, (xm_blk,), hist_m, l, *mw, (), _silu, sdims, "mconv_sample")
        cm = flat(cm_main, cm_meta, cm_samp)

        wif = jnp.pad(w_if[l], ((0, 0), (0, LANES - 2 * n_heads))).astype(BF16)
        wift = jnp.pad(w_if[l].T, ((0, SUBLANES - 2 * n_heads), (0, 0))).astype(BF16)
        bc = jnp.pad(b_if[l], (0, LANES - 2 * n_heads)).reshape(1, LANES)
        br = jnp.pad(b_if[l], (0, SUBLANES - 2 * n_heads)).reshape(SUBLANES, 1)
        q, k, v, gcol, grow = _qkv(cm, proj, xm_blk, *[w[l].reshape(-1, w.shape[-1]).T for w in (w_q, w_k, w_v)],
                                   wif, wift, bc, br)
        grow_meta = grow[:, m0:s0].reshape(SUBLANES, bsz, n_meta).transpose(1, 0, 2)
        grow_samp = grow[:, s0:t_real].reshape(SUBLANES, dbsz, dseq).transpose(1, 0, 2)
        hh_meta, hh_main, c_p, n_p, m_p = _mlstm_prompt(q, k, v, gcol, grow, grow_meta, pdims, n_heads)
        hh_samp, c_s, n_s, m_s = _mlstm_sample(
            q, k, v, gcol, grow_samp, c0_all, c_s, l, state_n[l],
            jnp.broadcast_to(state_m[l][:, :, None], (dbsz, n_heads, LANES)), sdims, n_heads)
        (yb,) = _rowwise(_down_body, t_pad, [(flat(hh_main, hh_meta, hh_samp), di, 0), (cm, di, 0), (proj, di, og_blk)],
                         [row(mh_ln_g[l]), row(skip[l]), _Layer(w_md_b, l)], [(d, F32)], "down")
        (h1,) = _rowwise(functools.partial(_merge_body, alpha), t_pad,
                         [(ya, d, 0), (yb, d, 0), (proj, d, ga_blk), (proj, d, ga_blk + 1), (x, d, 0)],
                         [_Layer(w_o_b, l), row(ln1_g[l]), row(ln1_b[l])], [(d, F32)], "merge")
        x = _moe(alpha, h1, w_router[l].T.astype(BF16), router_bias[l].reshape(-1, 1), we1, we3, we2, l,
                 w_s1[l].astype(BF16), w_s3[l].astype(BF16), w_s2[l].astype(BF16), row(ln2_g[l]), row(ln2_b[l]))

        outs["conv_p"].append(conv_p)
        outs["mconv_p"].append(mconv_p)
        outs["c_p"].append(c_p)
        outs["n_p"].append(n_p)
        outs["m_p"].append(m_p[:, :, 0])
        outs["conv_s"].append(conv_s)
        outs["mconv_s"].append(mconv_s)
        outs["n_s"].append(n_s)
        outs["m_s"].append(m_s[:, :, 0])

    y_prompt = x[:m0].reshape(bsz, seq, d)
    y_sample = x[s0:t_real].reshape(dbsz, dseq, d)
    st = {k: jnp.stack(v) for k, v in outs.items()}
    return (y_prompt, y_sample, st["conv_p"], st["mconv_p"], st["c_p"], st["n_p"], st["m_p"],
            st["conv_s"], st["mconv_s"], c_s.reshape(depth, dbsz, n_heads, hd, hd), st["n_s"], st["m_s"])
```

```python
import functools

import jax
import jax.numpy as jnp
from jax import lax
from jax.experimental import pallas as pl
from jax.experimental.pallas import tpu as pltpu

F32 = jnp.float32
BF16 = jnp.bfloat16
I32 = jnp.int32

LN_EPS = 1e-5
TOP_K = 8
N_GROUPS = 8
TOPK_GROUPS = 4
ROUTE_SCALE = 2.5

SUBLANES = 8
LANES = 128
MXU_DIM = 256
ROW_TILE = 512
SEQ_TILE = 256
EXPERT_ROWS = 512
VMEM_LIMIT = 48 * 1024 * 1024


def _round_up(x, m):
    return (x + m - 1) // m * m


def _params(sem):
    return pltpu.CompilerParams(dimension_semantics=sem, vmem_limit_bytes=VMEM_LIMIT)


def _ln(x, g, b):
    mu = jnp.mean(x, axis=-1, keepdims=True)
    xc = x - mu
    var = jnp.mean(xc * xc, axis=-1, keepdims=True)
    return xc * lax.rsqrt(var + LN_EPS) * g + b


def _silu(x):
    return x * jax.nn.sigmoid(x)


def _log_sigmoid(x):
    return jnp.minimum(x, 0.0) - jnp.log1p(jnp.exp(-jnp.abs(x)))


def _dot(a, b):
    return jnp.dot(a, b, preferred_element_type=F32)


def _dot_nt(a, b):
    return lax.dot_general(a, b, (((1,), (1,)), ((), ())), preferred_element_type=F32)


def _dot_tn(a, b):
    return lax.dot_general(a, b, (((0,), (0,)), ((), ())), preferred_element_type=F32)


class _Layer:
    def __init__(self, array, index):
        self.array, self.index = array, index


def _full_spec(a):
    if isinstance(a, _Layer):
        nd, l = a.array.ndim, a.index
        return pl.BlockSpec((1,) + a.array.shape[1:], lambda *_: (l,) + (0,) * (nd - 1))
    nd = a.ndim
    return pl.BlockSpec(a.shape, lambda *_: (0,) * nd)


def _arr(a):
    return a.array if isinstance(a, _Layer) else a


def _rowwise(body, rows, ins, consts, outs, name):
    tm = ROW_TILE
    in_specs = [pl.BlockSpec((tm, w), functools.partial(lambda i, cb: (i, cb), cb=cb)) for _, w, cb in ins]
    in_specs += [_full_spec(c) for c in consts]
    out_specs = [pl.BlockSpec((tm, w), lambda i: (i, 0)) for w, _ in outs]
    out_shape = [jax.ShapeDtypeStruct((rows, w), dt) for w, dt in outs]
    return pl.pallas_call(
        body, out_shape=out_shape, grid=(rows // tm,), in_specs=in_specs, out_specs=out_specs,
        compiler_params=_params(("arbitrary",)), name=name,
    )(*[a for a, _, _ in ins], *[_arr(c) for c in consts])


def _ln_body(x_ref, g_ref, b_ref, o_ref):
    o_ref[...] = _ln(x_ref[...], g_ref[...], b_ref[...])


def _in_proj_body(x_ref, w_ref, o_ref):
    o_ref[...] = _dot(x_ref[...].astype(BF16), w_ref[0]).astype(o_ref.dtype)


def _in_proj(x, w, l):
    rows, d = x.shape
    n = w.shape[2]
    tm, tn = ROW_TILE, min(4096, n)
    return pl.pallas_call(
        _in_proj_body, out_shape=jax.ShapeDtypeStruct((rows, n), F32), grid=(n // tn, rows // tm),
        in_specs=[pl.BlockSpec((tm, d), lambda j, i: (i, 0)), pl.BlockSpec((1, d, tn), lambda j, i: (l, 0, j))],
        out_specs=pl.BlockSpec((tm, tn), lambda j, i: (i, j)),
        compiler_params=_params(("arbitrary", "arbitrary")), name="in_proj",
    )(x, w)


def _qkv_body(cm_ref, xm_ref, wq_ref, wk_ref, wv_ref, wif_ref, wift_ref, bc_ref, br_ref,
              q_ref, k_ref, v_ref, gc_ref, gr_ref, wbd_ref):
    di = cm_ref.shape[1]
    nb = di // MXU_DIM
    qb = wq_ref.shape[0]

    @pl.when(pl.program_id(0) == 0)
    def _():
        c = lax.broadcasted_iota(I32, (MXU_DIM, MXU_DIM), 0)
        r = lax.broadcasted_iota(I32, (MXU_DIM, MXU_DIM), 1)
        same_block = (r // qb) == (c // qb)
        masks = [jnp.logical_and(same_block, (c % qb) == o) for o in range(qb)]
        for s, w_ref in enumerate((wq_ref, wk_ref, wv_ref)):
            for j in range(nb):
                cols = w_ref[:, j * MXU_DIM:(j + 1) * MXU_DIM]
                tile_t = functools.reduce(jnp.add, [jnp.where(masks[o], cols[o:o + 1, :], 0.0) for o in range(qb)])
                wbd_ref[s, j] = tile_t.T.astype(BF16)

    gc = jnp.zeros(gc_ref.shape, F32) + bc_ref[...]
    gr = jnp.zeros(gr_ref.shape, F32) + br_ref[...]
    for j in range(nb):
        sl = slice(j * MXU_DIM, (j + 1) * MXU_DIM)
        cmj = cm_ref[:, sl].astype(BF16)
        xmj = xm_ref[:, sl].astype(BF16)
        for o_ref, src, seg in ((q_ref, cmj, 0), (k_ref, cmj, 1), (v_ref, xmj, 2)):
            y = _dot(src, wbd_ref[seg, j])
            o_ref[:, sl] = y
            yb = y.astype(BF16)
            ws = slice(seg * di + j * MXU_DIM, seg * di + (j + 1) * MXU_DIM)
            gc = gc + _dot(yb, wif_ref[ws, :])
            gr = gr + _dot_nt(wift_ref[:, ws], yb)
    gc_ref[...] = gc
    gr_ref[...] = gr


def _qkv(cm, proj, xm_block, wq, wk, wv, wif, wift, bc, br):
    rows, di = cm.shape
    tm = ROW_TILE
    row_spec = lambda w, cb: pl.BlockSpec((tm, w), functools.partial(lambda i, cb: (i, cb), cb=cb))
    consts = (wq, wk, wv, wif, wift, bc, br)
    return pl.pallas_call(
        _qkv_body,
        out_shape=[jax.ShapeDtypeStruct((rows, di), F32)] * 3
        + [jax.ShapeDtypeStruct((rows, LANES), F32), jax.ShapeDtypeStruct((SUBLANES, rows), F32)],
        grid=(rows // tm,),
        in_specs=[row_spec(di, 0), row_spec(di, xm_block)] + [_full_spec(c) for c in consts],
        out_specs=[row_spec(di, 0)] * 3 + [row_spec(LANES, 0), pl.BlockSpec((SUBLANES, tm), lambda i: (0, i))],
        scratch_shapes=[pltpu.VMEM((3, di // MXU_DIM, MXU_DIM, MXU_DIM), BF16)],
        compiler_params=_params(("arbitrary",)), name="qkv_gates",
    )(cm, proj, *consts)


def _down_body(hh_ref, cm_ref, og_ref, g_ref, skip_ref, w_ref, o_ref):
    hb = (hh_ref[...] * g_ref[...] + skip_ref[...] * cm_ref[...]) * jax.nn.sigmoid(og_ref[...].astype(F32))
    o_ref[...] = _dot(hb.astype(BF16), w_ref[0])


def _merge_body(alpha, ya_ref, yb_ref, ga_ref, gb_ref, x_ref, w_ref, g_ref, b_ref, o_ref):
    merged = (jax.nn.sigmoid(ga_ref[...].astype(F32)) * ya_ref[...]
              + jax.nn.sigmoid(gb_ref[...].astype(F32)) * yb_ref[...])
    o = _dot(merged.astype(BF16), w_ref[0])
    o_ref[...] = _ln(alpha * x_ref[...] + o, g_ref[...], b_ref[...])


def _glu_pre(a, b):
    return a * jax.nn.sigmoid(b)


def _conv_taps(k_taps):
    hp = _round_up(k_taps - 1, SUBLANES)
    return hp, hp - (k_taps - 1)


def _conv_window(s_ref, w_ref, bias, n, k_taps):
    _, off = _conv_taps(k_taps)
    acc = jnp.zeros((n, s_ref.shape[1]), F32) + bias
    for k in range(k_taps):
        acc = acc + w_ref[k:k + 1, :] * s_ref[k + off:k + off + n, :]
    return acc


def _conv_long(s_ref, sh_ref, w_ref, bias, out_ref, post, n, k_taps):
    hp, off = _conv_taps(k_taps)
    c = s_ref.shape[1]
    shifts = sorted({(k + off) % SUBLANES for k in range(k_taps)} - {0})
    span = n + hp - SUBLANES
    for s in shifts:
        sh_ref[s, 0:span, :] = s_ref[s:s + span, :]
    rg = SUBLANES * max(1, min(4, 32 * 1024 // (SUBLANES * c)))
    while n % rg:
        rg //= 2

    def body(i, carry):
        t0 = pl.multiple_of(i * rg, rg)
        acc = jnp.zeros((rg, c), F32) + bias
        for k in range(k_taps):
            g, s = divmod(k + off, SUBLANES)
            if s == 0:
                x = s_ref[pl.ds(t0 + SUBLANES * g, rg), :]
            else:
                x = sh_ref[s, pl.ds(t0 + SUBLANES * g, rg), :]
            acc = acc + w_ref[k:k + 1, :] * x
        out_ref[pl.ds(t0, rg), :] = post(acc)
        return carry

    lax.fori_loop(0, n // rg, body, 0)


def _split_consts(consts, has_proj):
    return (consts[:-1], consts[-1]) if has_proj else (consts, None)


def _project(out_ref, wo_ref):
    if wo_ref is not None:
        out_ref[...] = _dot(out_ref[...].astype(BF16), wo_ref[0])


def _conv_prompt_body(n_pre, k_taps, post, bsz, has_proj, *refs):
    main = refs[:n_pre]
    meta = refs[n_pre:2 * n_pre]
    w_ref, b_ref = refs[2 * n_pre:2 * n_pre + 2]
    extra, wo_ref = _split_consts(refs[2 * n_pre + 2:-5], has_proj)
    out_main, out_meta, state_ref, s_ref, sh_ref = refs[-5:]
    pre = _glu_pre if n_pre == 2 else (lambda a: a)
    post_fn = functools.partial(post, *[e[...] for e in extra]) if extra else post
    hp, off = _conv_taps(k_taps)
    n_meta, tl = out_meta.shape[0], out_main.shape[0]
    c = pl.program_id(1)
    real = pl.program_id(0) < bsz
    bias = b_ref[...]

    @pl.when(jnp.logical_and(real, c == 0))
    def _():
        s_ref[0:hp, :] = jnp.zeros((hp, s_ref.shape[1]), F32)
        s_ref[hp:hp + n_meta, :] = pre(*[r[...].astype(F32) for r in meta])
        out_meta[...] = post_fn(_conv_window(s_ref, w_ref, bias, n_meta, k_taps))
        _project(out_meta, wo_ref)
        s_ref[0:hp, :] = s_ref[n_meta:n_meta + hp, :]

    @pl.when(jnp.logical_and(real, c > 0))
    def _():
        s_ref[hp:hp + tl, :] = pre(*[r[...].astype(F32) for r in main])
        _conv_long(s_ref, sh_ref, w_ref, bias, out_main, post_fn, tl, k_taps)
        _project(out_main, wo_ref)
        s_ref[0:hp, :] = s_ref[tl:tl + hp, :]

    @pl.when(jnp.logical_and(real, c == pl.num_programs(1) - 1))
    def _():
        state_ref[0] = s_ref[off:hp, :]

    @pl.when(jnp.logical_and(jnp.logical_not(real), c > 0))
    def _():
        out_main[...] = jnp.zeros(out_main.shape, F32)


def _prompt_rows(bsz, seq, t_pad):
    tl = min(SEQ_TILE, seq)
    nc = seq // tl
    blocks = t_pad // tl
    extra = -(-(blocks - bsz * nc) // nc)
    block = lambda bi, ci: jnp.minimum(bi * nc + jnp.maximum(ci - 1, 0), blocks - 1)
    return tl, nc, extra, block


def _conv_prompt(proj, col_blocks, w, b, extra, post, dims, name, w_out=None):
    bsz, seq, n_meta, t_pad = dims
    k_taps, c = w.shape
    hp, _ = _conv_taps(k_taps)
    tl, nc, n_extra, block = _prompt_rows(bsz, seq, t_pad)
    meta_blk0 = bsz * seq // n_meta
    n_pre = len(col_blocks)
    seq_i = lambda bi: jnp.minimum(bi, bsz - 1)
    main_map = lambda cb: (lambda bi, ci: (block(bi, ci), cb))
    meta_map = lambda cb: (lambda bi, ci: (meta_blk0 + seq_i(bi), cb))
    in_specs = [pl.BlockSpec((tl, c), main_map(cb)) for cb in col_blocks]
    in_specs += [pl.BlockSpec((n_meta, c), meta_map(cb)) for cb in col_blocks]
    consts = (w, b) + tuple(extra) + ((w_out,) if w_out is not None else ())
    in_specs += [_full_spec(a) for a in consts]
    return pl.pallas_call(
        functools.partial(_conv_prompt_body, n_pre, k_taps, post, bsz, w_out is not None),
        out_shape=[jax.ShapeDtypeStruct((t_pad, c), F32), jax.ShapeDtypeStruct((bsz * n_meta, c), F32),
                   jax.ShapeDtypeStruct((bsz, k_taps - 1, c), F32)],
        grid=(bsz + n_extra, 1 + nc), in_specs=in_specs,
        out_specs=[pl.BlockSpec((tl, c), main_map(0)), pl.BlockSpec((n_meta, c), lambda bi, ci: (seq_i(bi), 0)),
                   pl.BlockSpec((1, k_taps - 1, c), lambda bi, ci: (seq_i(bi), 0, 0))],
        scratch_shapes=[pltpu.VMEM((hp + tl, c), F32), pltpu.VMEM((SUBLANES, hp + tl, c), F32)],
        compiler_params=_params(("arbitrary", "arbitrary")), name=name,
    )(*([proj] * (2 * n_pre)), *[_arr(a) for a in consts])


def _conv_sample_body(n_pre, k_taps, post, seqs, has_proj, *refs):
    rows = refs[:n_pre]
    hist_ref, w_ref, b_ref = refs[n_pre:n_pre + 3]
    extra, wo_ref = _split_consts(refs[n_pre + 3:-4], has_proj)
    out_ref, state_ref, s_ref, u_ref = refs[-4:]
    pre = _glu_pre if n_pre == 2 else (lambda a: a)
    post_fn = functools.partial(post, *[e[...] for e in extra]) if extra else post
    hp, off = _conv_taps(k_taps)
    n = out_ref.shape[0] // seqs
    bias = b_ref[...]
    u_ref[...] = pre(*[r[...].astype(F32) for r in rows])

    def body(j, carry):
        r0 = pl.multiple_of(j * n, n)
        s_ref[off:hp, :] = hist_ref[j]
        s_ref[hp:hp + n, :] = u_ref[pl.ds(r0, n), :]
        out_ref[pl.ds(r0, n), :] = post_fn(_conv_window(s_ref, w_ref, bias, n, k_taps))
        state_ref[j] = s_ref[off + n:hp + n, :]
        return carry

    lax.fori_loop(0, seqs, body, 0)
    _project(out_ref, wo_ref)


def _conv_sample(proj, col_blocks, hist, layer, w, b, extra, post, dims, name, w_out=None):
    row0, dbsz, n = dims
    k_taps, c = w.shape
    hp, _ = _conv_taps(k_taps)
    seqs = min(SUBLANES, dbsz)
    blk0 = row0 // (seqs * n)
    hist0 = layer * (dbsz // seqs)
    n_pre = len(col_blocks)
    consts = (w, b) + tuple(extra) + ((w_out,) if w_out is not None else ())
    in_specs = [pl.BlockSpec((seqs * n, c), functools.partial(lambda i, cb: (blk0 + i, cb), cb=cb)) for cb in col_blocks]
    in_specs += [pl.BlockSpec((seqs, k_taps - 1, c), lambda i: (hist0 + i, 0, 0))]
    in_specs += [_full_spec(a) for a in consts]
    return pl.pallas_call(
        functools.partial(_conv_sample_body, n_pre, k_taps, post, seqs, w_out is not None),
        out_shape=[jax.ShapeDtypeStruct((dbsz * n, c), F32), jax.ShapeDtypeStruct((dbsz, k_taps - 1, c), F32)],
        grid=(dbsz // seqs,), in_specs=in_specs,
        out_specs=[pl.BlockSpec((seqs * n, c), lambda i: (i, 0)), pl.BlockSpec((seqs, k_taps - 1, c), lambda i: (i, 0, 0))],
        scratch_shapes=[pltpu.VMEM((hp + n, c), F32), pltpu.VMEM((seqs * n, c), F32)],
        compiler_params=_params(("arbitrary",)), name=name,
    )(*([proj] * n_pre), hist, *[_arr(a) for a in consts])


def _post_ln_silu(g, b, acc):
    return _silu(_ln(acc, g, b))


def _mlstm_chunk(gc, gr, q, k, v, c0, n0, m0, hsel, n_heads):
    seq, hd = q.shape
    lane = lax.broadcasted_iota(I32, gc.shape, 1)
    sub = lax.broadcasted_iota(I32, gr.shape, 0)
    pick_c = lambda j: jnp.sum(jnp.where(lane == j, gc, 0.0), axis=1, keepdims=True)
    pick_r = lambda j: jnp.sum(jnp.where(sub == j, gr, 0.0), axis=0, keepdims=True)
    ig_c, lf_c = pick_c(hsel), _log_sigmoid(pick_c(hsel + n_heads))
    ig_r, lf_r = pick_r(hsel), _log_sigmoid(pick_r(hsel + n_heads))
    ti = lax.broadcasted_iota(I32, (seq, seq), 0)
    si = lax.broadcasted_iota(I32, (seq, seq), 1)
    causal = si <= ti
    b_c = jnp.sum(jnp.where(causal, lf_r, 0.0), axis=1, keepdims=True)
    b_r = jnp.sum(jnp.where(ti <= si, lf_c, 0.0), axis=0, keepdims=True)
    log_w = jnp.where(causal, b_c - b_r + ig_r, -jnp.inf)
    log_s = b_c + m0
    m_c = jnp.maximum(log_s, jnp.max(log_w, axis=1, keepdims=True))
    w = jnp.exp(log_w - m_c)
    s_c = jnp.exp(log_s - m_c)
    ks = k * (hd ** -0.5)
    qb, kb, vb = q.astype(BF16), ks.astype(BF16), v.astype(BF16)
    qk = _dot_nt(qb, kb) * w
    num = s_c * _dot(qb, c0.astype(BF16)) + _dot(qk.astype(BF16), vb)
    den = s_c * jnp.sum(q * n0, axis=1, keepdims=True) + jnp.sum(qk, axis=1, keepdims=True)
    hval = num / jnp.maximum(jnp.abs(den), jnp.exp(-m_c))
    mu = jnp.mean(hval, axis=1, keepdims=True)
    hc = hval - mu
    var = jnp.mean(hc * hc, axis=1, keepdims=True)
    hh = hc * lax.rsqrt(var + LN_EPS)
    m_end = m_c[seq - 1:seq, :]
    b_last = b_c[seq - 1:seq, :]
    w_end = jnp.exp(b_last - b_c + ig_c - m_end)
    s_end = jnp.exp(b_last + m0 - m_end)
    kw = ks * w_end
    c1 = s_end * c0 + _dot_tn(kw.astype(BF16), vb)
    n1 = s_end * n0 + jnp.sum(kw, axis=0, keepdims=True)
    return hh, c1, n1, m_end


def _mlstm_prompt_body(n_heads, bsz, gcm_ref, grm_ref, qm_ref, km_ref, vm_ref, gc_ref, gr_ref, q_ref, k_ref, v_ref,
                       hm_ref, h_ref, c_out, n_out, m_out, c_s, n_s, m_s):
    c = pl.program_id(1)
    real = pl.program_id(0) < bsz
    hd = q_ref.shape[1] // n_heads

    def step(gc, gr, q, k, v, out):
        for h in range(n_heads):
            cols = slice(h * hd, (h + 1) * hd)
            hh, c1, n1, m1 = _mlstm_chunk(gc, gr, q[:, cols], k[:, cols], v[:, cols],
                                          c_s[h], n_s[h:h + 1, :], m_s[h:h + 1, 0:1], h, n_heads)
            out[:, cols] = hh
            c_s[h] = c1
            n_s[h:h + 1, :] = n1
            m_s[h:h + 1, :] = jnp.broadcast_to(m1, (1, m_s.shape[1]))

    @pl.when(jnp.logical_and(real, c == 0))
    def _():
        c_s[...] = jnp.zeros(c_s.shape, F32)
        n_s[...] = jnp.zeros(n_s.shape, F32)
        m_s[...] = jnp.zeros(m_s.shape, F32)
        step(gcm_ref[...], grm_ref[0], qm_ref, km_ref, vm_ref, hm_ref)

    @pl.when(jnp.logical_and(real, c > 0))
    def _():
        step(gc_ref[...], gr_ref[...], q_ref, k_ref, v_ref, h_ref)

    @pl.when(jnp.logical_and(real, c == pl.num_programs(1) - 1))
    def _():
        c_out[0] = c_s[...]
        n_out[0] = n_s[...]
        m_out[0] = m_s[...]

    @pl.when(jnp.logical_and(jnp.logical_not(real), c > 0))
    def _():
        h_ref[...] = jnp.zeros(h_ref.shape, F32)


def _mlstm_prompt(q, k, v, gcol, grow, grow_meta, dims, n_heads):
    bsz, seq, n_meta, t_pad = dims
    di = q.shape[1]
    hd = di // n_heads
    tl, nc, n_extra, block = _prompt_rows(bsz, seq, t_pad)
    meta_blk0 = bsz * seq // n_meta
    seq_i = lambda bi: jnp.minimum(bi, bsz - 1)
    qkv_main = pl.BlockSpec((tl, di), lambda bi, ci: (block(bi, ci), 0))
    qkv_meta = pl.BlockSpec((n_meta, di), lambda bi, ci: (meta_blk0 + seq_i(bi), 0))
    st = lambda shape: pl.BlockSpec((1,) + shape, lambda bi, ci: (seq_i(bi),) + (0,) * len(shape))
    return pl.pallas_call(
        functools.partial(_mlstm_prompt_body, n_heads, bsz),
        out_shape=[jax.ShapeDtypeStruct((bsz * n_meta, di), F32), jax.ShapeDtypeStruct((t_pad, di), F32),
                   jax.ShapeDtypeStruct((bsz, n_heads, hd, hd), F32), jax.ShapeDtypeStruct((bsz, n_heads, hd), F32),
                   jax.ShapeDtypeStruct((bsz, n_heads, LANES), F32)],
        grid=(bsz + n_extra, 1 + nc),
        in_specs=[pl.BlockSpec((n_meta, LANES), lambda bi, ci: (meta_blk0 + seq_i(bi), 0)),
                  pl.BlockSpec((1, SUBLANES, n_meta), lambda bi, ci: (seq_i(bi), 0, 0)),
                  qkv_meta, qkv_meta, qkv_meta,
                  pl.BlockSpec((tl, LANES), lambda bi, ci: (block(bi, ci), 0)),
                  pl.BlockSpec((SUBLANES, tl), lambda bi, ci: (0, block(bi, ci))),
                  qkv_main, qkv_main, qkv_main],
        out_specs=[pl.BlockSpec((n_meta, di), lambda bi, ci: (seq_i(bi), 0)), qkv_main,
                   st((n_heads, hd, hd)), st((n_heads, hd)), st((n_heads, LANES))],
        scratch_shapes=[pltpu.VMEM((n_heads, hd, hd), F32), pltpu.VMEM((n_heads, hd), F32),
                        pltpu.VMEM((n_heads, LANES), F32)],
        compiler_params=_params(("arbitrary",) * 2), name="mlstm_prompt",
    )(gcol, grow_meta, q, k, v, gcol, grow, q, k, v)


def _mlstm_sample_body(n_heads, has_acc, gc_ref, gr_ref, q_ref, k_ref, v_ref, c0_ref, n0_ref, m0_ref, *refs):
    h_ref, c_out, n_out, m_out = refs[1:] if has_acc else refs
    hd = q_ref.shape[1] // n_heads
    gc, gr = gc_ref[...], gr_ref[0]
    for h in range(n_heads):
        cols = slice(h * hd, (h + 1) * hd)
        hh, c1, n1, m1 = _mlstm_chunk(gc, gr, q_ref[:, cols], k_ref[:, cols], v_ref[:, cols],
                                      c0_ref[0, h], n0_ref[0, h:h + 1, :], m0_ref[0, h:h + 1, 0:1], h, n_heads)
        h_ref[:, cols] = hh
        c_out[0, h] = c1
        n_out[0, h:h + 1, :] = n1
        m_out[0, h:h + 1, :] = jnp.broadcast_to(m1, (1, m_out.shape[2]))


def _mlstm_sample(q, k, v, gcol, grow_s, c0_all, c_acc, layer, n0, m0, dims, n_heads):
    row0, dbsz, n = dims
    di = q.shape[1]
    hd = di // n_heads
    blk0 = row0 // n
    st0 = layer * dbsz
    qkv = pl.BlockSpec((n, di), lambda bi: (blk0 + bi, 0))
    c_spec = pl.BlockSpec((1, n_heads, hd, hd), lambda bi: (st0 + bi, 0, 0, 0))
    small = lambda w: pl.BlockSpec((1, n_heads, w), lambda bi: (bi, 0, 0))
    has_acc = c_acc is not None
    operands = (gcol, grow_s, q, k, v, c0_all, n0, m0) + ((c_acc,) if has_acc else ())
    return pl.pallas_call(
        functools.partial(_mlstm_sample_body, n_heads, has_acc),
        out_shape=[jax.ShapeDtypeStruct((dbsz * n, di), F32), jax.ShapeDtypeStruct(c0_all.shape, F32),
                   jax.ShapeDtypeStruct((dbsz, n_heads, hd), F32), jax.ShapeDtypeStruct((dbsz, n_heads, LANES), F32)],
        grid=(dbsz,),
        in_specs=[pl.BlockSpec((n, LANES), lambda bi: (blk0 + bi, 0)),
                  pl.BlockSpec((1, SUBLANES, n), lambda bi: (bi, 0, 0)),
                  qkv, qkv, qkv, c_spec, small(hd), small(LANES)]
        + ([pl.BlockSpec(memory_space=pl.ANY)] if has_acc else []),
        out_specs=[pl.BlockSpec((n, di), lambda bi: (bi, 0)), c_spec, small(hd), small(LANES)],
        input_output_aliases={8: 1} if has_acc else {},
        compiler_params=_params(("arbitrary",)), name="mlstm_sample",
    )(*operands)


def _stack_rows(rows, n):
    sub = lax.broadcasted_iota(I32, (len(rows), n), 0)
    out = jnp.zeros((len(rows), n), rows[0].dtype)
    for j, r in enumerate(rows):
        out = jnp.where(sub == j, r, out)
    return out


def _router_body(h_ref, wt_ref, bias_ref, te_ref, wk_ref):
    n_exp = wt_ref.shape[0]
    gsz = n_exp // N_GROUPS
    tm = h_ref.shape[0]
    logits = _dot_nt(wt_ref[...], h_ref[...].astype(BF16))
    scores = jax.nn.sigmoid(logits)
    choice = scores + bias_ref[...]
    sub = lax.broadcasted_iota(I32, (gsz, tm), 0)
    neg = -jnp.inf

    def first_max(x):
        m = jnp.max(x, axis=0, keepdims=True)
        idx = jnp.min(jnp.where(x == m, sub, gsz), axis=0, keepdims=True)
        return m, idx

    sc = [scores[g * gsz:(g + 1) * gsz, :] for g in range(N_GROUPS)]
    ch = [choice[g * gsz:(g + 1) * gsz, :] for g in range(N_GROUPS)]
    gs = []
    for x in ch:
        m1, i1 = first_max(x)
        m2, _ = first_max(jnp.where(sub == i1, neg, x))
        gs.append(m1 + m2)
    y = _stack_rows(gs, tm)
    gsub = lax.broadcasted_iota(I32, (N_GROUPS, tm), 0)
    gsel = jnp.zeros((N_GROUPS, tm), jnp.bool_)
    for _ in range(TOPK_GROUPS):
        m = jnp.max(y, axis=0, keepdims=True)
        idx = jnp.min(jnp.where(y == m, gsub, N_GROUPS), axis=0, keepdims=True)
        hit = gsub == idx
        gsel = jnp.logical_or(gsel, hit)
        y = jnp.where(hit, neg, y)
    gself = gsel.astype(F32)
    ch = [jnp.where(gself[g:g + 1, :] > 0.0, ch[g], neg) for g in range(N_GROUPS)]
    eid = [sub + g * gsz for g in range(N_GROUPS)]
    picks, pick_w = [], []
    for _ in range(TOP_K):
        m = functools.reduce(jnp.maximum, [jnp.max(x, axis=0, keepdims=True) for x in ch])
        idx = functools.reduce(jnp.minimum, [jnp.min(jnp.where(x == m, e, n_exp), axis=0, keepdims=True)
                                             for x, e in zip(ch, eid)])
        hits = [e == idx for e in eid]
        pick_w.append(functools.reduce(jnp.add, [jnp.sum(jnp.where(h, s, 0.0), axis=0, keepdims=True)
                                                 for h, s in zip(hits, sc)]))
        ch = [jnp.where(h, neg, x) for h, x in zip(hits, ch)]
        picks.append(idx)
    total = functools.reduce(jnp.add, pick_w)
    te_ref[...] = _stack_rows(picks, tm)
    wk_ref[...] = _stack_rows([w / total * ROUTE_SCALE for w in pick_w], tm)


def _router(h, wt, bias):
    rows, d = h.shape
    tm = ROW_TILE
    return pl.pallas_call(
        _router_body,
        out_shape=[jax.ShapeDtypeStruct((TOP_K, rows), I32), jax.ShapeDtypeStruct((TOP_K, rows), F32)],
        grid=(rows // tm,),
        in_specs=[pl.BlockSpec((tm, d), lambda i: (i, 0)), _full_spec(wt), _full_spec(bias)],
        out_specs=[pl.BlockSpec((TOP_K, tm), lambda i: (0, i))] * 2,
        compiler_params=_params(("arbitrary",)), name="router",
    )(h, wt, bias)


def _dispatch_body(n_blocks, te_ref, dest_ref, blke_ref, used_ref, start_ref, cnt_ref, cnt_s, start_s, carry_s):
    n_exp = cnt_s.shape[0]
    tw = te_ref.shape[1]
    phase, i = pl.program_id(0), pl.program_id(1)
    te = te_ref[...]
    eio = lax.broadcasted_iota(I32, (n_exp, tw), 0)
    sel = functools.reduce(jnp.logical_or, [eio == te[k:k + 1, :] for k in range(TOP_K)])
    self = sel.astype(F32)

    @pl.when(jnp.logical_and(phase == 0, i == 0))
    def _():
        cnt_s[...] = jnp.zeros(cnt_s.shape, F32)

    @pl.when(phase == 0)
    def _():
        cnt_s[...] += jnp.sum(self, axis=1, keepdims=True)

    @pl.when(jnp.logical_and(phase == 1, i == 0))
    def _():
        cnt = cnt_s[...]
        nblk = jnp.floor((cnt + (EXPERT_ROWS - 1)) * (1.0 / EXPERT_ROWS))
        r = lax.broadcasted_iota(I32, (n_exp, n_exp), 0)
        c = lax.broadcasted_iota(I32, (n_exp, n_exp), 1)
        lower = (c < r).astype(BF16)
        blk_start = _dot(lower, nblk.astype(BF16))
        blk_end = blk_start + nblk
        start_s[...] = blk_start * EXPERT_ROWS
        carry_s[...] = jnp.zeros(carry_s.shape, F32)
        start_ref[...] = (blk_start * EXPERT_ROWS).astype(I32)
        cnt_ref[...] = cnt.astype(I32)
        used_ref[...] = jnp.sum(nblk, axis=0, keepdims=True).astype(I32)
        jio = lax.broadcasted_iota(I32, (n_exp, blke_ref.shape[1]), 1).astype(F32)
        be = jnp.sum((blk_end[:, 0:1] <= jio).astype(F32), axis=0, keepdims=True)
        blke_ref[...] = jnp.minimum(be, n_exp - 1.0).astype(I32)

    @pl.when(phase == 1)
    def _():
        r = lax.broadcasted_iota(I32, (tw, tw), 0)
        c = lax.broadcasted_iota(I32, (tw, tw), 1)
        upper = (r <= c).astype(BF16)
        incl = _dot(self.astype(BF16), upper)
        dest = start_s[:, 0:1] + carry_s[:, 0:1] + incl - self
        carry_s[...] += jnp.sum(self, axis=1, keepdims=True)
        rows = [jnp.sum(jnp.where(eio == te[k:k + 1, :], dest, 0.0), axis=0, keepdims=True) for k in range(TOP_K)]
        dest_ref[...] = _stack_rows(rows, tw).astype(I32)


def _dispatch(te, n_exp, n_blocks):
    rows = te.shape[1]
    tw = ROW_TILE
    nbp = _round_up(n_blocks, LANES)
    small = lambda shape: pl.BlockSpec(shape, lambda p, i: (0, 0))
    return pl.pallas_call(
        functools.partial(_dispatch_body, n_blocks),
        out_shape=[jax.ShapeDtypeStruct((TOP_K, rows), I32), jax.ShapeDtypeStruct((1, nbp), I32),
                   jax.ShapeDtypeStruct((1, LANES), I32), jax.ShapeDtypeStruct((n_exp, LANES), I32),
                   jax.ShapeDtypeStruct((n_exp, LANES), I32)],
        grid=(2, rows // tw),
        in_specs=[pl.BlockSpec((TOP_K, tw), lambda p, i: (0, i))],
        out_specs=[pl.BlockSpec((TOP_K, tw), lambda p, i: (0, i * p)), small((1, nbp)), small((1, LANES)),
                   small((n_exp, LANES)), small((n_exp, LANES))],
        scratch_shapes=[pltpu.VMEM((n_exp, LANES), F32)] * 3,
        compiler_params=_params(("arbitrary", "arbitrary")), name="dispatch",
    )(te)


def _to_tiles(ref, x, rows=slice(None)):
    ref[rows] = x.reshape((x.shape[0],) + ref.shape[-2:])


def _from_tiles(ref, lead=(), rows=slice(None)):
    x = ref[lead + (rows,)]
    return x.reshape(x.shape[0], x.shape[1] * x.shape[2])


def _scatter_body(start_ref, cnt_ref, used_ref, dest_ref, h_ref, xs_ref, h3_ref, zero_ref, sem):
    tm = h_ref.shape[0]
    _to_tiles(h3_ref, h_ref[...].astype(h3_ref.dtype))
    row_copy = lambda t, r: pltpu.make_async_copy(h3_ref.at[t], xs_ref.at[r], sem)

    def issue(t, carry):
        for k in range(TOP_K):
            row_copy(t, dest_ref[k, t]).start(priority=k % 2)
        return carry

    lax.fori_loop(0, tm, issue, 0)
    for k in range(TOP_K):
        pltpu.make_async_copy(h3_ref, xs_ref.at[pl.ds(0, tm)], sem).wait()

    @pl.when(pl.program_id(0) == pl.num_programs(0) - 1)
    def _():
        zero_ref[...] = jnp.zeros(zero_ref.shape, zero_ref.dtype)
        sizes = [1 << b for b in range(EXPERT_ROWS.bit_length() - 1)]

        def fill(e, carry):
            cnt = cnt_ref[e]
            pad = (-cnt) & (EXPERT_ROWS - 1)
            row = start_ref[e] + cnt
            for sz in sizes:
                cp = pltpu.make_async_copy(zero_ref.at[pl.ds(0, sz)], xs_ref.at[pl.ds(row, sz)], sem)

                @pl.when((pad & sz) != 0)
                def _():
                    cp.start()
                    cp.wait()

                row = row + (pad & sz)
            return carry

        lax.fori_loop(0, start_ref.shape[0], fill, 0)

        def fill_block(j, carry):
            for r0 in range(0, EXPERT_ROWS, zero_ref.shape[0]):
                cp = pltpu.make_async_copy(zero_ref, xs_ref.at[pl.ds(j * EXPERT_ROWS + r0, zero_ref.shape[0])], sem)
                cp.start()
                cp.wait()
            return carry

        lax.fori_loop(used_ref[0], xs_ref.shape[0] // EXPERT_ROWS, fill_block, 0)


def _scatter(h, dest, start, cnt, used, n_blocks):
    rows, d = h.shape
    tm = ROW_TILE // 2
    lt = d // LANES
    grid_spec = pltpu.PrefetchScalarGridSpec(
        num_scalar_prefetch=3, grid=(rows // tm,),
        in_specs=[pl.BlockSpec((TOP_K, tm), lambda i, *_: (0, i), memory_space=pltpu.SMEM),
                  pl.BlockSpec((tm, d), lambda i, *_: (i, 0))],
        out_specs=pl.BlockSpec(memory_space=pl.ANY),
        scratch_shapes=[pltpu.VMEM((tm, lt, LANES), BF16), pltpu.VMEM((EXPERT_ROWS // 2, lt, LANES), BF16),
                        pltpu.SemaphoreType.DMA],
    )
    return pl.pallas_call(
        _scatter_body, out_shape=jax.ShapeDtypeStruct((n_blocks * EXPERT_ROWS, lt, LANES), BF16), grid_spec=grid_spec,
        compiler_params=_params(("arbitrary",)), name="moe_scatter",
    )(start, cnt, used, dest, h)


def _expert_body(blke_ref, used_ref, xs_ref, w1_ref, w3_ref, w2_ref, ys_ref, w1_s, w3_s, w2_s):
    j = pl.program_id(0)
    prev = blke_ref[jnp.maximum(j - 1, 0)]
    live = j < used_ref[0]

    @pl.when(jnp.logical_and(live, jnp.logical_or(j == 0, blke_ref[j] != prev)))
    def _():
        w1_s[...] = w1_ref[0].astype(BF16)
        w3_s[...] = w3_ref[0].astype(BF16)
        w2_s[...] = w2_ref[0].astype(BF16)

    @pl.when(live)
    def _():
        half = EXPERT_ROWS // 2
        for r0 in (0, half):
            rows = slice(r0, r0 + half)
            x = _from_tiles(xs_ref, rows=rows).astype(BF16)
            mid = _silu(_dot(x, w1_s[...])) * _dot(x, w3_s[...])
            _to_tiles(ys_ref, _dot(mid.astype(BF16), w2_s[...]), rows)

    @pl.when(jnp.logical_not(live))
    def _():
        ys_ref[...] = jnp.zeros(ys_ref.shape, F32)


def _experts(xs, blke, used, w1, w3, w2, e0):
    rows, lt, _ = xs.shape
    d, de = w1.shape[1:]
    n_blocks = rows // EXPERT_ROWS
    blk = lambda j, blke_ref, used_ref: jnp.minimum(j, used_ref[0] - 1)
    xmap = lambda j, b, u: (blk(j, b, u), 0, 0)
    wmap = lambda j, b, u: (e0 + b[blk(j, b, u)], 0, 0)
    grid_spec = pltpu.PrefetchScalarGridSpec(
        num_scalar_prefetch=2, grid=(n_blocks,),
        in_specs=[pl.BlockSpec((EXPERT_ROWS, lt, LANES), xmap), pl.BlockSpec((1, d, de), wmap),
                  pl.BlockSpec((1, d, de), wmap), pl.BlockSpec((1, de, d), wmap)],
        out_specs=pl.BlockSpec((EXPERT_ROWS, lt, LANES), lambda j, b, u: (j, 0, 0)),
        scratch_shapes=[pltpu.VMEM((d, de), BF16), pltpu.VMEM((d, de), BF16), pltpu.VMEM((de, d), BF16)],
    )
    return pl.pallas_call(
        _expert_body, out_shape=jax.ShapeDtypeStruct(xs.shape, F32), grid_spec=grid_spec,
        compiler_params=_params(("arbitrary",)), name="moe_experts",
    )(blke, used, xs, w1, w3, w2)


def _combine_body(alpha, dest_ref, wk_ref, h_ref, ys_ref, ws1_ref, ws3_ref, ws2_ref, g_ref, b_ref, o_ref, buf, sem):
    tm = h_ref.shape[0]
    row_copy = lambda k, t, r: pltpu.make_async_copy(ys_ref.at[r], buf.at[k, t], sem)

    def issue(t, carry):
        for k in range(TOP_K):
            row_copy(k, t, dest_ref[k, t]).start(priority=k % 2)
        return carry

    lax.fori_loop(0, tm, issue, 0)
    h = h_ref[...]
    hb = h.astype(BF16)
    f = _dot((_silu(_dot(hb, ws1_ref[...])) * _dot(hb, ws3_ref[...])).astype(BF16), ws2_ref[...])
    for k in range(TOP_K):
        pltpu.make_async_copy(ys_ref.at[pl.ds(0, tm)], buf.at[k], sem).wait()
    for k in range(TOP_K):
        f = f + wk_ref[:, k:k + 1] * _from_tiles(buf, (k,))
    o_ref[...] = _ln(alpha * h + f, g_ref[...], b_ref[...])


def _combine(alpha, h, ys, dest, wk_col, ws1, ws3, ws2, g, b):
    rows, d = h.shape
    tm = ROW_TILE // 4
    consts = (ws1, ws3, ws2, g, b)
    return pl.pallas_call(
        functools.partial(_combine_body, alpha), out_shape=jax.ShapeDtypeStruct((rows, d), F32), grid=(rows // tm,),
        in_specs=[pl.BlockSpec((TOP_K, tm), lambda i: (0, i), memory_space=pltpu.SMEM),
                  pl.BlockSpec((tm, TOP_K), lambda i: (i, 0)), pl.BlockSpec((tm, d), lambda i: (i, 0)),
                  pl.BlockSpec(memory_space=pl.ANY)] + [_full_spec(c) for c in consts],
        out_specs=pl.BlockSpec((tm, d), lambda i: (i, 0)),
        scratch_shapes=[pltpu.VMEM((TOP_K, tm, d // LANES, LANES), F32), pltpu.SemaphoreType.DMA],
        compiler_params=_params(("arbitrary",)), name="moe_combine",
    )(dest, wk_col, h, ys, *consts)


def _moe(alpha, h, wrt, rbias, w1, w3, w2, layer, ws1, ws3, ws2, g, b):
    rows = h.shape[0]
    n_exp = wrt.shape[0]
    n_blocks = -(-rows * TOP_K // EXPERT_ROWS) + n_exp
    te, wk = _router(h, wrt, rbias)
    dest, blke, used, start, cnt = _dispatch(te, n_exp, n_blocks)
    xs = _scatter(h, dest, start[:, 0], cnt[:, 0], used[0, :1], n_blocks)
    ys = _experts(xs, blke[0, :n_blocks], used[0, :1], w1, w3, w2, layer * n_exp)
    return _combine(alpha, h, ys, dest, wk.T, ws1, ws3, ws2, g, b)


def kernel(x_prompt, x_sample, state_conv, state_mconv, state_C, state_n, state_m, meta_tokens, ln_in_g, ln_in_b, w_in, conv_w, conv_b, conv_ln_g, conv_ln_b, w_conv_out, mconv_w, mconv_b, w_q, w_k, w_v, w_if, b_if, mh_ln_g, skip, w_m_down, w_o, ln1_g, ln1_b, ln2_g, ln2_b, w_router, router_bias, w_e1, w_e3, w_e2, w_s1, w_s3, w_s2):
    bsz, seq, d = x_prompt.shape
    dbsz, dseq, _ = x_sample.shape
    depth = w_in.shape[0]
    n_meta = meta_tokens.shape[0]
    dc = conv_w.shape[2]
    di = mconv_w.shape[2]
    n_heads = state_m.shape[2]
    hd = di // n_heads
    alpha = (2 * depth) ** 0.25
    m0 = bsz * seq
    s0 = m0 + bsz * n_meta
    t_real = s0 + dbsz * dseq
    t_pad = _round_up(t_real, ROW_TILE)
    assert dc == d and di % dc == 0 and m0 % n_meta == 0 and s0 % (SUBLANES * dseq) == 0
    assert seq % min(SEQ_TILE, seq) == 0 and dbsz % SUBLANES == 0 and 2 * n_heads <= SUBLANES
    row = lambda a: a.reshape(1, -1)

    def flat(main, meta, samp):
        out = lax.dynamic_update_slice(main, meta, (m0, 0))
        return lax.dynamic_update_slice(out, samp, (s0, 0))

    x_all = jnp.concatenate([x_prompt.reshape(m0, d), jnp.tile(meta_tokens, (bsz, 1)), x_sample.reshape(dbsz * dseq, d),
                             jnp.zeros((t_pad - t_real, d), F32)], axis=0)
    (x,) = _rowwise(_ln_body, t_pad, [(x_all, d, 0)], [row(ln_in_g), row(ln_in_b)], [(d, F32)], "ln_in")

    xm_blk = 2 * dc // di
    og_blk = (2 * dc + di) // di
    ga_blk = (2 * dc + 2 * di) // d
    outs = {k: [] for k in ("conv_p", "mconv_p", "c_p", "n_p", "m_p", "conv_s", "mconv_s", "n_s", "m_s")}
    pdims, sdims = (bsz, seq, n_meta, t_pad), (s0, dbsz, dseq)
    w_in_b, w_co_b, w_md_b, w_o_b = (w.astype(BF16) for w in (w_in, w_conv_out, w_m_down, w_o))
    n_exp = w_e1.shape[1]
    we1, we3, we2 = (w.reshape((depth * n_exp,) + w.shape[2:]) for w in (w_e1, w_e3, w_e2))
    hist_c = state_conv.reshape((depth * dbsz,) + state_conv.shape[2:])
    hist_m = state_mconv.reshape((depth * dbsz,) + state_mconv.shape[2:])
    c0_all = state_C.reshape(depth * dbsz, n_heads, hd, hd)
    c_s = None
    for l in range(depth):
        proj = _in_proj(x, w_in_b, l)
        cw = (conv_w[l], row(conv_b[l]))
        cl = (row(conv_ln_g[l]), row(conv_ln_b[l]))
        wco = _Layer(w_co_b, l)
        a_main, a_meta, conv_p = _conv_prompt(proj, (0, 1), *cw, cl, _post_ln_silu, pdims, "conv_prompt", wco)
        a_samp, conv_s = _conv_sample(proj, (0, 1), hist_c, l, *cw, cl, _post_ln_silu, sdims, "conv_sample", wco)
        ya = flat(a_main, a_meta, a_samp)
        mw = (mconv_w[l], row(mconv_b[l]))
        cm_main, cm_meta, mconv_p = _conv_prompt(proj, (xm_blk,), *mw, (), _silu, pdims, "mconv_prompt")
        cm_samp, mconv_s = _conv_sample(proj, (xm_blk,), hist_m, l, *mw, (), _silu, sdims, "mconv_sample")
        cm = flat(cm_main, cm_meta, cm_samp)

        wif = jnp.pad(w_if[l], ((0, 0), (0, LANES - 2 * n_heads))).astype(BF16)
        wift = jnp.pad(w_if[l].T, ((0, SUBLANES - 2 * n_heads), (0, 0))).astype(BF16)
        bc = jnp.pad(b_if[l], (0, LANES - 2 * n_heads)).reshape(1, LANES)
        br = jnp.pad(b_if[l], (0, SUBLANES - 2 * n_heads)).reshape(SUBLANES, 1)
        q, k, v, gcol, grow = _qkv(cm, proj, xm_blk, *[w[l].reshape(-1, w.shape[-1]).T for w in (w_q, w_k, w_v)],
                                   wif, wift, bc, br)
        grow_meta = grow[:, m0:s0].reshape(SUBLANES, bsz, n_meta).transpose(1, 0, 2)
        grow_samp = grow[:, s0:t_real].reshape(SUBLANES, dbsz, dseq).transpose(1, 0, 2)
        hh_meta, hh_main, c_p, n_p, m_p = _mlstm_prompt(q, k, v, gcol, grow, grow_meta, pdims, n_heads)
        hh_samp, c_s, n_s, m_s = _mlstm_sample(
            q, k, v, gcol, grow_samp, c0_all, c_s, l, state_n[l],
            jnp.broadcast_to(state_m[l][:, :, None], (dbsz, n_heads, LANES)), sdims, n_heads)
        (yb,) = _rowwise(_down_body, t_pad, [(flat(hh_main, hh_meta, hh_samp), di, 0), (cm, di, 0), (proj, di, og_blk)],
                         [row(mh_ln_g[l]), row(skip[l]), _Layer(w_md_b, l)], [(d, F32)], "down")
        (h1,) = _rowwise(functools.partial(_merge_body, alpha), t_pad,
                         [(ya, d, 0), (yb, d, 0), (proj, d, ga_blk), (proj, d, ga_blk + 1), (x, d, 0)],
                         [_Layer(w_o_b, l), row(ln1_g[l]), row(ln1_b[l])], [(d, F32)], "merge")
        x = _moe(alpha, h1, w_router[l].T.astype(BF16), router_bias[l].reshape(-1, 1), we1, we3, we2, l,
                 w_s1[l].astype(BF16), w_s3[l].astype(BF16), w_s2[l].astype(BF16), row(ln2_g[l]), row(ln2_b[l]))

        outs["conv_p"].append(conv_p)
        outs["mconv_p"].append(mconv_p)
        outs["c_p"].append(c_p)
        outs["n_p"].append(n_p)
        outs["m_p"].append(m_p[:, :, 0])
        outs["conv_s"].append(conv_s)
        outs["mconv_s"].append(mconv_s)
        outs["n_s"].append(n_s)
        outs["m_s"].append(m_s[:, :, 0])

    y_prompt = x[:m0].reshape(bsz, seq, d)
    y_sample = x[s0:t_real].reshape(dbsz, dseq, d)
    st = {k: jnp.stack(v) for k, v in outs.items()}
    return (y_prompt, y_sample, st["conv_p"], st["mconv_p"], st["c_p"], st["n_p"], st["m_p"],
            st["conv_s"], st["mconv_s"], c_s.reshape(depth, dbsz, n_heads, hd, hd), st["n_s"], st["m_s"])
```

```python
import functools

import jax
import jax.numpy as jnp
from jax import lax
from jax.experimental import pallas as pl
from jax.experimental.pallas import tpu as pltpu

F32 = jnp.float32
BF16 = jnp.bfloat16
I32 = jnp.int32

LN_EPS = 1e-5
TOP_K = 8
N_GROUPS = 8
TOPK_GROUPS = 4
ROUTE_SCALE = 2.5

SUBLANES = 8
LANES = 128
MXU_DIM = 256
ROW_TILE = 512
SEQ_TILE = 256
EXPERT_ROWS = 512
VMEM_LIMIT = 48 * 1024 * 1024


def _round_up(x, m):
    return (x + m - 1) // m * m


def _params(sem):
    return pltpu.CompilerParams(dimension_semantics=sem, vmem_limit_bytes=VMEM_LIMIT)


def _ln(x, g, b):
    mu = jnp.mean(x, axis=-1, keepdims=True)
    xc = x - mu
    var = jnp.mean(xc * xc, axis=-1, keepdims=True)
    return xc * lax.rsqrt(var + LN_EPS) * g + b


def _silu(x):
    return x * jax.nn.sigmoid(x)


def _log_sigmoid(x):
    return jnp.minimum(x, 0.0) - jnp.log1p(jnp.exp(-jnp.abs(x)))


def _dot(a, b):
    return jnp.dot(a, b, preferred_element_type=F32)


def _dot_nt(a, b):
    return lax.dot_general(a, b, (((1,), (1,)), ((), ())), preferred_element_type=F32)


def _dot_tn(a, b):
    return lax.dot_general(a, b, (((0,), (0,)), ((), ())), preferred_element_type=F32)


class _Layer:
    def __init__(self, array, index):
        self.array, self.index = array, index


def _full_spec(a):
    if isinstance(a, _Layer):
        nd, l = a.array.ndim, a.index
        return pl.BlockSpec((1,) + a.array.shape[1:], lambda *_: (l,) + (0,) * (nd - 1))
    nd = a.ndim
    return pl.BlockSpec(a.shape, lambda *_: (0,) * nd)


def _arr(a):
    return a.array if isinstance(a, _Layer) else a


def _rowwise(body, rows, ins, consts, outs, name):
    tm = ROW_TILE
    in_specs = [pl.BlockSpec((tm, w), functools.partial(lambda i, cb: (i, cb), cb=cb)) for _, w, cb in ins]
    in_specs += [_full_spec(c) for c in consts]
    out_specs = [pl.BlockSpec((tm, w), lambda i: (i, 0)) for w, _ in outs]
    out_shape = [jax.ShapeDtypeStruct((rows, w), dt) for w, dt in outs]
    return pl.pallas_call(
        body, out_shape=out_shape, grid=(rows // tm,), in_specs=in_specs, out_specs=out_specs,
        compiler_params=_params(("arbitrary",)), name=name,
    )(*[a for a, _, _ in ins], *[_arr(c) for c in consts])


def _ln_body(x_ref, g_ref, b_ref, o_ref):
    o_ref[...] = _ln(x_ref[...], g_ref[...], b_ref[...])


def _in_proj_body(x_ref, w_ref, o_ref):
    o_ref[...] = _dot(x_ref[...].astype(BF16), w_ref[0]).astype(o_ref.dtype)


def _in_proj(x, w, l):
    rows, d = x.shape
    n = w.shape[2]
    tm, tn = ROW_TILE, min(4096, n)
    return pl.pallas_call(
        _in_proj_body, out_shape=jax.ShapeDtypeStruct((rows, n), F32), grid=(n // tn, rows // tm),
        in_specs=[pl.BlockSpec((tm, d), lambda j, i: (i, 0)), pl.BlockSpec((1, d, tn), lambda j, i: (l, 0, j))],
        out_specs=pl.BlockSpec((tm, tn), lambda j, i: (i, j)),
        compiler_params=_params(("arbitrary", "arbitrary")), name="in_proj",
    )(x, w)


def _qkv_body(cm_ref, xm_ref, wq_ref, wk_ref, wv_ref, wif_ref, wift_ref, bc_ref, br_ref,
              q_ref, k_ref, v_ref, gc_ref, gr_ref, wbd_ref):
    di = cm_ref.shape[1]
    nb = di // MXU_DIM
    qb = wq_ref.shape[0]

    @pl.when(pl.program_id(0) == 0)
    def _():
        c = lax.broadcasted_iota(I32, (MXU_DIM, MXU_DIM), 0)
        r = lax.broadcasted_iota(I32, (MXU_DIM, MXU_DIM), 1)
        same_block = (r // qb) == (c // qb)
        masks = [jnp.logical_and(same_block, (c % qb) == o) for o in range(qb)]
        for s, w_ref in enumerate((wq_ref, wk_ref, wv_ref)):
            for j in range(nb):
                cols = w_ref[:, j * MXU_DIM:(j + 1) * MXU_DIM]
                tile_t = functools.reduce(jnp.add, [jnp.where(masks[o], cols[o:o + 1, :], 0.0) for o in range(qb)])
                wbd_ref[s, j] = tile_t.T.astype(BF16)

    gc = jnp.zeros(gc_ref.shape, F32) + bc_ref[...]
    gr = jnp.zeros(gr_ref.shape, F32) + br_ref[...]
    for j in range(nb):
        sl = slice(j * MXU_DIM, (j + 1) * MXU_DIM)
        cmj = cm_ref[:, sl].astype(BF16)
        xmj = xm_ref[:, sl].astype(BF16)
        for o_ref, src, seg in ((q_ref, cmj, 0), (k_ref, cmj, 1), (v_ref, xmj, 2)):
            y = _dot(src, wbd_ref[seg, j])
            o_ref[:, sl] = y
            yb = y.astype(BF16)
            ws = slice(seg * di + j * MXU_DIM, seg * di + (j + 1) * MXU_DIM)
            gc = gc + _dot(yb, wif_ref[ws, :])
            gr = gr + _dot_nt(wift_ref[:, ws], yb)
    gc_ref[...] = gc
    gr_ref[...] = gr


def _qkv(cm, proj, xm_block, wq, wk, wv, wif, wift, bc, br):
    rows, di = cm.shape
    tm = ROW_TILE
    row_spec = lambda w, cb: pl.BlockSpec((tm, w), functools.partial(lambda i, cb: (i, cb), cb=cb))
    consts = (wq, wk, wv, wif, wift, bc, br)
    return pl.pallas_call(
        _qkv_body,
        out_shape=[jax.ShapeDtypeStruct((rows, di), F32)] * 3
        + [jax.ShapeDtypeStruct((rows, LANES), F32), jax.ShapeDtypeStruct((SUBLANES, rows), F32)],
        grid=(rows // tm,),
        in_specs=[row_spec(di, 0), row_spec(di, xm_block)] + [_full_spec(c) for c in consts],
        out_specs=[row_spec(di, 0)] * 3 + [row_spec(LANES, 0), pl.BlockSpec((SUBLANES, tm), lambda i: (0, i))],
        scratch_shapes=[pltpu.VMEM((3, di // MXU_DIM, MXU_DIM, MXU_DIM), BF16)],
        compiler_params=_params(("arbitrary",)), name="qkv_gates",
    )(cm, proj, *consts)


def _down_body(hh_ref, cm_ref, og_ref, g_ref, skip_ref, w_ref, o_ref):
    hb = (hh_ref[...] * g_ref[...] + skip_ref[...] * cm_ref[...]) * jax.nn.sigmoid(og_ref[...].astype(F32))
    o_ref[...] = _dot(hb.astype(BF16), w_ref[0])


def _merge_body(alpha, ya_ref, yb_ref, ga_ref, gb_ref, x_ref, w_ref, g_ref, b_ref, o_ref):
    merged = (jax.nn.sigmoid(ga_ref[...].astype(F32)) * ya_ref[...]
              + jax.nn.sigmoid(gb_ref[...].astype(F32)) * yb_ref[...])
    o = _dot(merged.astype(BF16), w_ref[0])
    o_ref[...] = _ln(alpha * x_ref[...] + o, g_ref[...], b_ref[...])


def _glu_pre(a, b):
    return a * jax.nn.sigmoid(b)


def _conv_taps(k_taps):
    hp = _round_up(k_taps - 1, SUBLANES)
    return hp, hp - (k_taps - 1)


def _conv_window(s_ref, w_ref, bias, n, k_taps):
    _, off = _conv_taps(k_taps)
    acc = jnp.zeros((n, s_ref.shape[1]), F32) + bias
    for k in range(k_taps):
        acc = acc + w_ref[k:k + 1, :] * s_ref[k + off:k + off + n, :]
    return acc


def _conv_long(s_ref, sh_ref, w_ref, bias, out_ref, post, n, k_taps):
    hp, off = _conv_taps(k_taps)
    c = s_ref.shape[1]
    shifts = sorted({(k + off) % SUBLANES for k in range(k_taps)} - {0})
    span = n + hp - SUBLANES
    for s in shifts:
        sh_ref[s, 0:span, :] = s_ref[s:s + span, :]
    rg = SUBLANES * max(1, min(4, 32 * 1024 // (SUBLANES * c)))
    while n % rg:
        rg //= 2

    def body(i, carry):
        t0 = pl.multiple_of(i * rg, rg)
        acc = jnp.zeros((rg, c), F32) + bias
        for k in range(k_taps):
            g, s = divmod(k + off, SUBLANES)
            if s == 0:
                x = s_ref[pl.ds(t0 + SUBLANES * g, rg), :]
            else:
                x = sh_ref[s, pl.ds(t0 + SUBLANES * g, rg), :]
            acc = acc + w_ref[k:k + 1, :] * x
        out_ref[pl.ds(t0, rg), :] = post(acc)
        return carry

    lax.fori_loop(0, n // rg, body, 0)


def _split_consts(consts, has_proj):
    return (consts[:-1], consts[-1]) if has_proj else (consts, None)


def _project(out_ref, wo_ref):
    if wo_ref is not None:
        out_ref[...] = _dot(out_ref[...].astype(BF16), wo_ref[0])


def _conv_prompt_body(n_pre, k_taps, post, bsz, has_proj, *refs):
    main = refs[:n_pre]
    meta = refs[n_pre:2 * n_pre]
    w_ref, b_ref = refs[2 * n_pre:2 * n_pre + 2]
    extra, wo_ref = _split_consts(refs[2 * n_pre + 2:-5], has_proj)
    out_main, out_meta, state_ref, s_ref, sh_ref = refs[-5:]
    pre = _glu_pre if n_pre == 2 else (lambda a: a)
    post_fn = functools.partial(post, *[e[...] for e in extra]) if extra else post
    hp, off = _conv_taps(k_taps)
    n_meta, tl = out_meta.shape[0], out_main.shape[0]
    c = pl.program_id(1)
    real = pl.program_id(0) < bsz
    bias = b_ref[...]

    @pl.when(jnp.logical_and(real, c == 0))
    def _():
        s_ref[0:hp, :] = jnp.zeros((hp, s_ref.shape[1]), F32)
        s_ref[hp:hp + n_meta, :] = pre(*[r[...].astype(F32) for r in meta])
        out_meta[...] = post_fn(_conv_window(s_ref, w_ref, bias, n_meta, k_taps))
        _project(out_meta, wo_ref)
        s_ref[0:hp, :] = s_ref[n_meta:n_meta + hp, :]

    @pl.when(jnp.logical_and(real, c > 0))
    def _():
        s_ref[hp:hp + tl, :] = pre(*[r[...].astype(F32) for r in main])
        _conv_long(s_ref, sh_ref, w_ref, bias, out_main, post_fn, tl, k_taps)
        _project(out_main, wo_ref)
        s_ref[0:hp, :] = s_ref[tl:tl + hp, :]

    @pl.when(jnp.logical_and(real, c == pl.num_programs(1) - 1))
    def _():
        state_ref[0] = s_ref[off:hp, :]

    @pl.when(jnp.logical_and(jnp.logical_not(real), c > 0))
    def _():
        out_main[...] = jnp.zeros(out_main.shape, F32)


def _prompt_rows(bsz, seq, t_pad):
    tl = min(SEQ_TILE, seq)
    nc = seq // tl
    blocks = t_pad // tl
    extra = -(-(blocks - bsz * nc) // nc)
    block = lambda bi, ci: jnp.minimum(bi * nc + jnp.maximum(ci - 1, 0), blocks - 1)
    return tl, nc, extra, block


def _conv_prompt(proj, col_blocks, w, b, extra, post, dims, name, w_out=None):
    bsz, seq, n_meta, t_pad = dims
    k_taps, c = w.shape
    hp, _ = _conv_taps(k_taps)
    tl, nc, n_extra, block = _prompt_rows(bsz, seq, t_pad)
    meta_blk0 = bsz * seq // n_meta
    n_pre = len(col_blocks)
    seq_i = lambda bi: jnp.minimum(bi, bsz - 1)
    main_map = lambda cb: (lambda bi, ci: (block(bi, ci), cb))
    meta_map = lambda cb: (lambda bi, ci: (meta_blk0 + seq_i(bi), cb))
    in_specs = [pl.BlockSpec((tl, c), main_map(cb)) for cb in col_blocks]
    in_specs += [pl.BlockSpec((n_meta, c), meta_map(cb)) for cb in col_blocks]
    consts = (w, b) + tuple(extra) + ((w_out,) if w_out is not None else ())
    in_specs += [_full_spec(a) for a in consts]
    return pl.pallas_call(
        functools.partial(_conv_prompt_body, n_pre, k_taps, post, bsz, w_out is not None),
        out_shape=[jax.ShapeDtypeStruct((t_pad, c), F32), jax.ShapeDtypeStruct((bsz * n_meta, c), F32),
                   jax.ShapeDtypeStruct((bsz, k_taps - 1, c), F32)],
        grid=(bsz + n_extra, 1 + nc), in_specs=in_specs,
        out_specs=[pl.BlockSpec((tl, c), main_map(0)), pl.BlockSpec((n_meta, c), lambda bi, ci: (seq_i(bi), 0)),
                   pl.BlockSpec((1, k_taps - 1, c), lambda bi, ci: (seq_i(bi), 0, 0))],
        scratch_shapes=[pltpu.VMEM((hp + tl, c), F32), pltpu.VMEM((SUBLANES, hp + tl, c), F32)],
        compiler_params=_params(("arbitrary", "arbitrary")), name=name,
    )(*([proj] * (2 * n_pre)), *[_arr(a) for a in consts])


def _conv_sample_body(n_pre, k_taps, post, seqs, has_proj, *refs):
    rows = refs[:n_pre]
    hist_ref, w_ref, b_ref = refs[n_pre:n_pre + 3]
    extra, wo_ref = _split_consts(refs[n_pre + 3:-4], has_proj)
    out_ref, state_ref, s_ref, u_ref = refs[-4:]
    pre = _glu_pre if n_pre == 2 else (lambda a: a)
    post_fn = functools.partial(post, *[e[...] for e in extra]) if extra else post
    hp, off = _conv_taps(k_taps)
    n = out_ref.shape[0] // seqs
    bias = b_ref[...]
    u_ref[...] = pre(*[r[...].astype(F32) for r in rows])

    def body(j, carry):
        r0 = pl.multiple_of(j * n, n)
        s_ref[off:hp, :] = hist_ref[j]
        s_ref[hp:hp + n, :] = u_ref[pl.ds(r0, n), :]
        out_ref[pl.ds(r0, n), :] = post_fn(_conv_window(s_ref, w_ref, bias, n, k_taps))
        state_ref[j] = s_ref[off + n:hp + n, :]
        return carry

    lax.fori_loop(0, seqs, body, 0)
    _project(out_ref, wo_ref)


def _conv_sample(proj, col_blocks, hist, layer, w, b, extra, post, dims, name, w_out=None):
    row0, dbsz, n = dims
    k_taps, c = w.shape
    hp, _ = _conv_taps(k_taps)
    seqs = min(SUBLANES, dbsz)
    blk0 = row0 // (seqs * n)
    hist0 = layer * (dbsz // seqs)
    n_pre = len(col_blocks)
    consts = (w, b) + tuple(extra) + ((w_out,) if w_out is not None else ())
    in_specs = [pl.BlockSpec((seqs * n, c), functools.partial(lambda i, cb: (blk0 + i, cb), cb=cb)) for cb in col_blocks]
    in_specs += [pl.BlockSpec((seqs, k_taps - 1, c), lambda i: (hist0 + i, 0, 0))]
    in_specs += [_full_spec(a) for a in consts]
    return pl.pallas_call(
        functools.partial(_conv_sample_body, n_pre, k_taps, post, seqs, w_out is not None),
        out_shape=[jax.ShapeDtypeStruct((dbsz * n, c), F32), jax.ShapeDtypeStruct((dbsz, k_taps - 1, c), F32)],
        grid=(dbsz // seqs,), in_specs=in_specs,
        out_specs=[pl.BlockSpec((seqs * n, c), lambda i: (i, 0)), pl.BlockSpec((seqs, k_taps - 1, c), lambda i: (i, 0, 0))],
        scratch_shapes=[pltpu.VMEM((hp + n, c), F32), pltpu.VMEM((seqs * n, c), F32)],
        compiler_params=_params(("arbitrary",)), name=name,
    )(*([proj] * n_pre), hist, *[_arr(a) for a in consts])


def _post_ln_silu(g, b, acc):
    return _silu(_ln(acc, g, b))


def _mlstm_chunk(gc, gr, q, k, v, c0, n0, m0, hsel, n_heads):
    seq, hd = q.shape
    lane = lax.broadcasted_iota(I32, gc.shape, 1)
    sub = lax.broadcasted_iota(I32, gr.shape, 0)
    pick_c = lambda j: jnp.sum(jnp.where(lane == j, gc, 0.0), axis=1, keepdims=True)
    pick_r = lambda j: jnp.sum(jnp.where(sub == j, gr, 0.0), axis=0, keepdims=True)
    ig_c, lf_c = pick_c(hsel), _log_sigmoid(pick_c(hsel + n_heads))
    ig_r, lf_r = pick_r(hsel), _log_sigmoid(pick_r(hsel + n_heads))
    ti = lax.broadcasted_iota(I32, (seq, seq), 0)
    si = lax.broadcasted_iota(I32, (seq, seq), 1)
    causal = si <= ti
    b_c = jnp.sum(jnp.where(causal, lf_r, 0.0), axis=1, keepdims=True)
    b_r = jnp.sum(jnp.where(ti <= si, lf_c, 0.0), axis=0, keepdims=True)
    log_w = jnp.where(causal, b_c - b_r + ig_r, -jnp.inf)
    log_s = b_c + m0
    m_c = jnp.maximum(log_s, jnp.max(log_w, axis=1, keepdims=True))
    w = jnp.exp(log_w - m_c)
    s_c = jnp.exp(log_s - m_c)
    ks = k * (hd ** -0.5)
    qb, kb, vb = q.astype(BF16), ks.astype(BF16), v.astype(BF16)
    qk = _dot_nt(qb, kb) * w
    num = s_c * _dot(qb, c0.astype(BF16)) + _dot(qk.astype(BF16), vb)
    den = s_c * jnp.sum(q * n0, axis=1, keepdims=True) + jnp.sum(qk, axis=1, keepdims=True)
    hval = num / jnp.maximum(jnp.abs(den), jnp.exp(-m_c))
    mu = jnp.mean(hval, axis=1, keepdims=True)
    hc = hval - mu
    var = jnp.mean(hc * hc, axis=1, keepdims=True)
    hh = hc * lax.rsqrt(var + LN_EPS)
    m_end = m_c[seq - 1:seq, :]
    b_last = b_c[seq - 1:seq, :]
    w_end = jnp.exp(b_last - b_c + ig_c - m_end)
    s_end = jnp.exp(b_last + m0 - m_end)
    kw = ks * w_end
    c1 = s_end * c0 + _dot_tn(kw.astype(BF16), vb)
    n1 = s_end * n0 + jnp.sum(kw, axis=0, keepdims=True)
    return hh, c1, n1, m_end


def _mlstm_prompt_body(n_heads, bsz, gcm_ref, grm_ref, qm_ref, km_ref, vm_ref, gc_ref, gr_ref, q_ref, k_ref, v_ref,
                       hm_ref, h_ref, c_out, n_out, m_out, c_s, n_s, m_s):
    c = pl.program_id(1)
    real = pl.program_id(0) < bsz
    hd = q_ref.shape[1] // n_heads

    def step(gc, gr, q, k, v, out):
        for h in range(n_heads):
            cols = slice(h * hd, (h + 1) * hd)
            hh, c1, n1, m1 = _mlstm_chunk(gc, gr, q[:, cols], k[:, cols], v[:, cols],
                                          c_s[h], n_s[h:h + 1, :], m_s[h:h + 1, 0:1], h, n_heads)
            out[:, cols] = hh
            c_s[h] = c1
            n_s[h:h + 1, :] = n1
            m_s[h:h + 1, :] = jnp.broadcast_to(m1, (1, m_s.shape[1]))

    @pl.when(jnp.logical_and(real, c == 0))
    def _():
        c_s[...] = jnp.zeros(c_s.shape, F32)
        n_s[...] = jnp.zeros(n_s.shape, F32)
        m_s[...] = jnp.zeros(m_s.shape, F32)
        step(gcm_ref[...], grm_ref[0], qm_ref, km_ref, vm_ref, hm_ref)

    @pl.when(jnp.logical_and(real, c > 0))
    def _():
        step(gc_ref[...], gr_ref[...], q_ref, k_ref, v_ref, h_ref)

    @pl.when(jnp.logical_and(real, c == pl.num_programs(1) - 1))
    def _():
        c_out[0] = c_s[...]
        n_out[0] = n_s[...]
        m_out[0] = m_s[...]

    @pl.when(jnp.logical_and(jnp.logical_not(real), c > 0))
    def _():
        h_ref[...] = jnp.zeros(h_ref.shape, F32)


def _mlstm_prompt(q, k, v, gcol, grow, grow_meta, dims, n_heads):
    bsz, seq, n_meta, t_pad = dims
    di = q.shape[1]
    hd = di // n_heads
    tl, nc, n_extra, block = _prompt_rows(bsz, seq, t_pad)
    meta_blk0 = bsz * seq // n_meta
    seq_i = lambda bi: jnp.minimum(bi, bsz - 1)
    qkv_main = pl.BlockSpec((tl, di), lambda bi, ci: (block(bi, ci), 0))
    qkv_meta = pl.BlockSpec((n_meta, di), lambda bi, ci: (meta_blk0 + seq_i(bi), 0))
    st = lambda shape: pl.BlockSpec((1,) + shape, lambda bi, ci: (seq_i(bi),) + (0,) * len(shape))
    return pl.pallas_call(
        functools.partial(_mlstm_prompt_body, n_heads, bsz),
        out_shape=[jax.ShapeDtypeStruct((bsz * n_meta, di), F32), jax.ShapeDtypeStruct((t_pad, di), F32),
                   jax.ShapeDtypeStruct((bsz, n_heads, hd, hd), F32), jax.ShapeDtypeStruct((bsz, n_heads, hd), F32),
                   jax.ShapeDtypeStruct((bsz, n_heads, LANES), F32)],
        grid=(bsz + n_extra, 1 + nc),
        in_specs=[pl.BlockSpec((n_meta, LANES), lambda bi, ci: (meta_blk0 + seq_i(bi), 0)),
                  pl.BlockSpec((1, SUBLANES, n_meta), lambda bi, ci: (seq_i(bi), 0, 0)),
                  qkv_meta, qkv_meta, qkv_meta,
                  pl.BlockSpec((tl, LANES), lambda bi, ci: (block(bi, ci), 0)),
                  pl.BlockSpec((SUBLANES, tl), lambda bi, ci: (0, block(bi, ci))),
                  qkv_main, qkv_main, qkv_main],
        out_specs=[pl.BlockSpec((n_meta, di), lambda bi, ci: (seq_i(bi), 0)), qkv_main,
                   st((n_heads, hd, hd)), st((n_heads, hd)), st((n_heads, LANES))],
        scratch_shapes=[pltpu.VMEM((n_heads, hd, hd), F32), pltpu.VMEM((n_heads, hd), F32),
                        pltpu.VMEM((n_heads, LANES), F32)],
        compiler_params=_params(("arbitrary",) * 2), name="mlstm_prompt",
    )(gcol, grow_meta, q, k, v, gcol, grow, q, k, v)


def _mlstm_sample_body(n_heads, has_acc, gc_ref, gr_ref, q_ref, k_ref, v_ref, c0_ref, n0_ref, m0_ref, *refs):
    h_ref, c_out, n_out, m_out = refs[1:] if has_acc else refs
    hd = q_ref.shape[1] // n_heads
    gc, gr = gc_ref[...], gr_ref[0]
    for h in range(n_heads):
        cols = slice(h * hd, (h + 1) * hd)
        hh, c1, n1, m1 = _mlstm_chunk(gc, gr, q_ref[:, cols], k_ref[:, cols], v_ref[:, cols],
                                      c0_ref[0, h], n0_ref[0, h:h + 1, :], m0_ref[0, h:h + 1, 0:1], h, n_heads)
        h_ref[:, cols] = hh
        c_out[0, h] = c1
        n_out[0, h:h + 1, :] = n1
        m_out[0, h:h + 1, :] = jnp.broadcast_to(m1, (1, m_out.shape[2]))


def _mlstm_sample(q, k, v, gcol, grow_s, c0_all, c_acc, layer, n0, m0, dims, n_heads):
    row0, dbsz, n = dims
    di = q.shape[1]
    hd = di // n_heads
    blk0 = row0 // n
    st0 = layer * dbsz
    qkv = pl.BlockSpec((n, di), lambda bi: (blk0 + bi, 0))
    c_spec = pl.BlockSpec((1, n_heads, hd, hd), lambda bi: (st0 + bi, 0, 0, 0))
    small = lambda w: pl.BlockSpec((1, n_heads, w), lambda bi: (bi, 0, 0))
    has_acc = c_acc is not None
    operands = (gcol, grow_s, q, k, v, c0_all, n0, m0) + ((c_acc,) if has_acc else ())
    return pl.pallas_call(
        functools.partial(_mlstm_sample_body, n_heads, has_acc),
        out_shape=[jax.ShapeDtypeStruct((dbsz * n, di), F32), jax.ShapeDtypeStruct(c0_all.shape, F32),
                   jax.ShapeDtypeStruct((dbsz, n_heads, hd), F32), jax.ShapeDtypeStruct((dbsz, n_heads, LANES), F32)],
        grid=(dbsz,),
        in_specs=[pl.BlockSpec((n, LANES), lambda bi: (blk0 + bi, 0)),
                  pl.BlockSpec((1, SUBLANES, n), lambda bi: (bi, 0, 0)),
                  qkv, qkv, qkv, c_spec, small(hd), small(LANES)]
        + ([pl.BlockSpec(memory_space=pl.ANY)] if has_acc else []),
        out_specs=[pl.BlockSpec((n, di), lambda bi: (bi, 0)), c_spec, small(hd), small(LANES)],
        input_output_aliases={8: 1} if has_acc else {},
        compiler_params=_params(("arbitrary",)), name="mlstm_sample",
    )(*operands)


def _stack_rows(rows, n):
    sub = lax.broadcasted_iota(I32, (len(rows), n), 0)
    out = jnp.zeros((len(rows), n), rows[0].dtype)
    for j, r in enumerate(rows):
        out = jnp.where(sub == j, r, out)
    return out


def _router_body(h_ref, wt_ref, bias_ref, te_ref, wk_ref):
    n_exp = wt_ref.shape[0]
    gsz = n_exp // N_GROUPS
    tm = h_ref.shape[0]
    logits = _dot_nt(wt_ref[...], h_ref[...].astype(BF16))
    scores = jax.nn.sigmoid(logits)
    choice = scores + bias_ref[...]
    sub = lax.broadcasted_iota(I32, (gsz, tm), 0)
    neg = -jnp.inf

    def first_max(x):
        m = jnp.max(x, axis=0, keepdims=True)
        idx = jnp.min(jnp.where(x == m, sub, gsz), axis=0, keepdims=True)
        return m, idx

    sc = [scores[g * gsz:(g + 1) * gsz, :] for g in range(N_GROUPS)]
    ch = [choice[g * gsz:(g + 1) * gsz, :] for g in range(N_GROUPS)]
    gs = []
    for x in ch:
        m1, i1 = first_max(x)
        m2, _ = first_max(jnp.where(sub == i1, neg, x))
        gs.append(m1 + m2)
    y = _stack_rows(gs, tm)
    gsub = lax.broadcasted_iota(I32, (N_GROUPS, tm), 0)
    gsel = jnp.zeros((N_GROUPS, tm), jnp.bool_)
    for _ in range(TOPK_GROUPS):
        m = jnp.max(y, axis=0, keepdims=True)
        idx = jnp.min(jnp.where(y == m, gsub, N_GROUPS), axis=0, keepdims=True)
        hit = gsub == idx
        gsel = jnp.logical_or(gsel, hit)
        y = jnp.where(hit, neg, y)
    gself = gsel.astype(F32)
    ch = [jnp.where(gself[g:g + 1, :] > 0.0, ch[g], neg) for g in range(N_GROUPS)]
    eid = [sub + g * gsz for g in range(N_GROUPS)]
    picks, pick_w = [], []
    for _ in range(TOP_K):
        m = functools.reduce(jnp.maximum, [jnp.max(x, axis=0, keepdims=True) for x in ch])
        idx = functools.reduce(jnp.minimum, [jnp.min(jnp.where(x == m, e, n_exp), axis=0, keepdims=True)
                                             for x, e in zip(ch, eid)])
        hits = [e == idx for e in eid]
        pick_w.append(functools.reduce(jnp.add, [jnp.sum(jnp.where(h, s, 0.0), axis=0, keepdims=True)
                                                 for h, s in zip(hits, sc)]))
        ch = [jnp.where(h, neg, x) for h, x in zip(hits, ch)]
        picks.append(idx)
    total = functools.reduce(jnp.add, pick_w)
    te_ref[...] = _stack_rows(picks, tm)
    wk_ref[...] = _stack_rows([w / total * ROUTE_SCALE for w in pick_w], tm)


def _router(h, wt, bias):
    rows, d = h.shape
    tm = ROW_TILE
    return pl.pallas_call(
        _router_body,
        out_shape=[jax.ShapeDtypeStruct((TOP_K, rows), I32), jax.ShapeDtypeStruct((TOP_K, rows), F32)],
        grid=(rows // tm,),
        in_specs=[pl.BlockSpec((tm, d), lambda i: (i, 0)), _full_spec(wt), _full_spec(bias)],
        out_specs=[pl.BlockSpec((TOP_K, tm), lambda i: (0, i))] * 2,
        compiler_params=_params(("arbitrary",)), name="router",
    )(h, wt, bias)


def _dispatch_body(n_blocks, te_ref, dest_ref, blke_ref, used_ref, start_ref, cnt_ref, cnt_s, start_s, carry_s):
    n_exp = cnt_s.shape[0]
    tw = te_ref.shape[1]
    phase, i = pl.program_id(0), pl.program_id(1)
    te = te_ref[...]
    eio = lax.broadcasted_iota(I32, (n_exp, tw), 0)
    sel = functools.reduce(jnp.logical_or, [eio == te[k:k + 1, :] for k in range(TOP_K)])
    self = sel.astype(F32)

    @pl.when(jnp.logical_and(phase == 0, i == 0))
    def _():
        cnt_s[...] = jnp.zeros(cnt_s.shape, F32)

    @pl.when(phase == 0)
    def _():
        cnt_s[...] += jnp.sum(self, axis=1, keepdims=True)

    @pl.when(jnp.logical_and(phase == 1, i == 0))
    def _():
        cnt = cnt_s[...]
        nblk = jnp.floor((cnt + (EXPERT_ROWS - 1)) * (1.0 / EXPERT_ROWS))
        r = lax.broadcasted_iota(I32, (n_exp, n_exp), 0)
        c = lax.broadcasted_iota(I32, (n_exp, n_exp), 1)
        lower = (c < r).astype(BF16)
        blk_start = _dot(lower, nblk.astype(BF16))
        blk_end = blk_start + nblk
        start_s[...] = blk_start * EXPERT_ROWS
        carry_s[...] = jnp.zeros(carry_s.shape, F32)
        start_ref[...] = (blk_start * EXPERT_ROWS).astype(I32)
        cnt_ref[...] = cnt.astype(I32)
        used_ref[...] = jnp.sum(nblk, axis=0, keepdims=True).astype(I32)
        jio = lax.broadcasted_iota(I32, (n_exp, blke_ref.shape[1]), 1).astype(F32)
        be = jnp.sum((blk_end[:, 0:1] <= jio).astype(F32), axis=0, keepdims=True)
        blke_ref[...] = jnp.minimum(be, n_exp - 1.0).astype(I32)

    @pl.when(phase == 1)
    def _():
        r = lax.broadcasted_iota(I32, (tw, tw), 0)
        c = lax.broadcasted_iota(I32, (tw, tw), 1)
        upper = (r <= c).astype(BF16)
        incl = _dot(self.astype(BF16), upper)
        dest = start_s[:, 0:1] + carry_s[:, 0:1] + incl - self
        carry_s[...] += jnp.sum(self, axis=1, keepdims=True)
        rows = [jnp.sum(jnp.where(eio == te[k:k + 1, :], dest, 0.0), axis=0, keepdims=True) for k in range(TOP_K)]
        dest_ref[...] = _stack_rows(rows, tw).astype(I32)


def _dispatch(te, n_exp, n_blocks):
    rows = te.shape[1]
    tw = ROW_TILE
    nbp = _round_up(n_blocks, LANES)
    small = lambda shape: pl.BlockSpec(shape, lambda p, i: (0, 0))
    return pl.pallas_call(
        functools.partial(_dispatch_body, n_blocks),
        out_shape=[jax.ShapeDtypeStruct((TOP_K, rows), I32), jax.ShapeDtypeStruct((1, nbp), I32),
                   jax.ShapeDtypeStruct((1, LANES), I32), jax.ShapeDtypeStruct((n_exp, LANES), I32),
                   jax.ShapeDtypeStruct((n_exp, LANES), I32)],
        grid=(2, rows // tw),
        in_specs=[pl.BlockSpec((TOP_K, tw), lambda p, i: (0, i))],
        out_specs=[pl.BlockSpec((TOP_K, tw), lambda p, i: (0, i * p)), small((1, nbp)), small((1, LANES)),
                   small((n_exp, LANES)), small((n_exp, LANES))],
        scratch_shapes=[pltpu.VMEM((n_exp, LANES), F32)] * 3,
        compiler_params=_params(("arbitrary", "arbitrary")), name="dispatch",
    )(te)


def _to_tiles(ref, x, rows=slice(None)):
    ref[rows] = x.reshape((x.shape[0],) + ref.shape[-2:])


def _from_tiles(ref, lead=(), rows=slice(None)):
    x = ref[lead + (rows,)]
    return x.reshape(x.shape[0], x.shape[1] * x.shape[2])


def _scatter_body(start_ref, cnt_ref, used_ref, dest_ref, h_ref, xs_ref, h3_ref, zero_ref, sem):
    tm = h_ref.shape[0]
    _to_tiles(h3_ref, h_ref[...].astype(h3_ref.dtype))
    row_copy = lambda t, r: pltpu.make_async_copy(h3_ref.at[t], xs_ref.at[r], sem)

    def issue(t, carry):
        for k in range(TOP_K):
            row_copy(t, dest_ref[k, t]).start(priority=k % 2)
        return carry

    lax.fori_loop(0, tm, issue, 0)
    for k in range(TOP_K):
        pltpu.make_async_copy(h3_ref, xs_ref.at[pl.ds(0, tm)], sem).wait()

    @pl.when(pl.program_id(0) == pl.num_programs(0) - 1)
    def _():
        zero_ref[...] = jnp.zeros(zero_ref.shape, zero_ref.dtype)
        sizes = [1 << b for b in range(EXPERT_ROWS.bit_length() - 1)]

        def fill(e, carry):
            cnt = cnt_ref[e]
            pad = (-cnt) & (EXPERT_ROWS - 1)
            row = start_ref[e] + cnt
            for sz in sizes:
                cp = pltpu.make_async_copy(zero_ref.at[pl.ds(0, sz)], xs_ref.at[pl.ds(row, sz)], sem)

                @pl.when((pad & sz) != 0)
                def _():
                    cp.start()
                    cp.wait()

                row = row + (pad & sz)
            return carry

        lax.fori_loop(0, start_ref.shape[0], fill, 0)

        def fill_block(j, carry):
            for r0 in range(0, EXPERT_ROWS, zero_ref.shape[0]):
                cp = pltpu.make_async_copy(zero_ref, xs_ref.at[pl.ds(j * EXPERT_ROWS + r0, zero_ref.shape[0])], sem)
                cp.start()
                cp.wait()
            return carry

        lax.fori_loop(used_ref[0], xs_ref.shape[0] // EXPERT_ROWS, fill_block, 0)


def _scatter(h, dest, start, cnt, used, n_blocks):
    rows, d = h.shape
    tm = ROW_TILE
    lt = d // LANES
    grid_spec = pltpu.PrefetchScalarGridSpec(
        num_scalar_prefetch=3, grid=(rows // tm,),
        in_specs=[pl.BlockSpec((TOP_K, tm), lambda i, *_: (0, i), memory_space=pltpu.SMEM),
                  pl.BlockSpec((tm, d), lambda i, *_: (i, 0))],
        out_specs=pl.BlockSpec(memory_space=pl.ANY),
        scratch_shapes=[pltpu.VMEM((tm, lt, LANES), BF16), pltpu.VMEM((EXPERT_ROWS // 2, lt, LANES), BF16),
                        pltpu.SemaphoreType.DMA],
    )
    return pl.pallas_call(
        _scatter_body, out_shape=jax.ShapeDtypeStruct((n_blocks * EXPERT_ROWS, lt, LANES), BF16), grid_spec=grid_spec,
        compiler_params=_params(("arbitrary",)), name="moe_scatter",
    )(start, cnt, used, dest, h)


def _expert_body(blke_ref, used_ref, xs_ref, w1_ref, w3_ref, w2_ref, ys_ref, w1_s, w3_s, w2_s):
    j = pl.program_id(0)
    prev = blke_ref[jnp.maximum(j - 1, 0)]
    live = j < used_ref[0]

    @pl.when(jnp.logical_and(live, jnp.logical_or(j == 0, blke_ref[j] != prev)))
    def _():
        w1_s[...] = w1_ref[0].astype(BF16)
        w3_s[...] = w3_ref[0].astype(BF16)
        w2_s[...] = w2_ref[0].astype(BF16)

    @pl.when(live)
    def _():
        half = EXPERT_ROWS // 2
        for r0 in (0, half):
            rows = slice(r0, r0 + half)
            x = _from_tiles(xs_ref, rows=rows).astype(BF16)
            mid = _silu(_dot(x, w1_s[...])) * _dot(x, w3_s[...])
            _to_tiles(ys_ref, _dot(mid.astype(BF16), w2_s[...]).astype(ys_ref.dtype), rows)

    @pl.when(jnp.logical_not(live))
    def _():
        ys_ref[...] = jnp.zeros(ys_ref.shape, ys_ref.dtype)


def _experts(xs, blke, used, w1, w3, w2, e0):
    rows, lt, _ = xs.shape
    d, de = w1.shape[1:]
    n_blocks = rows // EXPERT_ROWS
    blk = lambda j, blke_ref, used_ref: jnp.minimum(j, used_ref[0] - 1)
    xmap = lambda j, b, u: (blk(j, b, u), 0, 0)
    wmap = lambda j, b, u: (e0 + b[blk(j, b, u)], 0, 0)
    grid_spec = pltpu.PrefetchScalarGridSpec(
        num_scalar_prefetch=2, grid=(n_blocks,),
        in_specs=[pl.BlockSpec((EXPERT_ROWS, lt, LANES), xmap), pl.BlockSpec((1, d, de), wmap),
                  pl.BlockSpec((1, d, de), wmap), pl.BlockSpec((1, de, d), wmap)],
        out_specs=pl.BlockSpec((EXPERT_ROWS, lt, LANES), lambda j, b, u: (j, 0, 0)),
        scratch_shapes=[pltpu.VMEM((d, de), BF16), pltpu.VMEM((d, de), BF16), pltpu.VMEM((de, d), BF16)],
    )
    return pl.pallas_call(
        _expert_body, out_shape=jax.ShapeDtypeStruct(xs.shape, BF16), grid_spec=grid_spec,
        compiler_params=_params(("arbitrary",)), name="moe_experts",
    )(blke, used, xs, w1, w3, w2)


def _combine_body(alpha, dest_ref, wk_ref, h_ref, ys_ref, ws1_ref, ws3_ref, ws2_ref, g_ref, b_ref, o_ref, buf, sem):
    tm = h_ref.shape[0]
    row_copy = lambda k, t, r: pltpu.make_async_copy(ys_ref.at[r], buf.at[k, t], sem)

    def issue(t, carry):
        for k in range(TOP_K):
            row_copy(k, t, dest_ref[k, t]).start(priority=k % 2)
        return carry

    lax.fori_loop(0, tm, issue, 0)
    h = h_ref[...]
    hb = h.astype(BF16)
    f = _dot((_silu(_dot(hb, ws1_ref[...])) * _dot(hb, ws3_ref[...])).astype(BF16), ws2_ref[...])
    for k in range(TOP_K):
        pltpu.make_async_copy(ys_ref.at[pl.ds(0, tm)], buf.at[k], sem).wait()
    for k in range(TOP_K):
        f = f + wk_ref[:, k:k + 1] * _from_tiles(buf, (k,)).astype(F32)
    o_ref[...] = _ln(alpha * h + f, g_ref[...], b_ref[...])


def _combine(alpha, h, ys, dest, wk_col, ws1, ws3, ws2, g, b):
    rows, d = h.shape
    tm = ROW_TILE
    consts = (ws1, ws3, ws2, g, b)
    return pl.pallas_call(
        functools.partial(_combine_body, alpha), out_shape=jax.ShapeDtypeStruct((rows, d), F32), grid=(rows // tm,),
        in_specs=[pl.BlockSpec((TOP_K, tm), lambda i: (0, i), memory_space=pltpu.SMEM),
                  pl.BlockSpec((tm, TOP_K), lambda i: (i, 0)), pl.BlockSpec((tm, d), lambda i: (i, 0)),
                  pl.BlockSpec(memory_space=pl.ANY)] + [_full_spec(c) for c in consts],
        out_specs=pl.BlockSpec((tm, d), lambda i: (i, 0)),
        scratch_shapes=[pltpu.VMEM((TOP_K, tm, d // LANES, LANES), ys.dtype), pltpu.SemaphoreType.DMA],
        compiler_params=_params(("arbitrary",)), name="moe_combine",
    )(dest, wk_col, h, ys, *consts)


def _moe(alpha, h, wrt, rbias, w1, w3, w2, layer, ws1, ws3, ws2, g, b):
    rows = h.shape[0]
    n_exp = wrt.shape[0]
    n_blocks = -(-rows * TOP_K // EXPERT_ROWS) + n_exp
    te, wk = _router(h, wrt, rbias)
    dest, blke, used, start, cnt = _dispatch(te, n_exp, n_blocks)
    xs = _scatter(h, dest, start[:, 0], cnt[:, 0], used[0, :1], n_blocks)
    ys = _experts(xs, blke[0, :n_blocks], used[0, :1], w1, w3, w2, layer * n_exp)
    return _combine(alpha, h, ys, dest, wk.T, ws1, ws3, ws2, g, b)


def kernel(x_prompt, x_sample, state_conv, state_mconv, state_C, state_n, state_m, meta_tokens, ln_in_g, ln_in_b, w_in, conv_w, conv_b, conv_ln_g, conv_ln_b, w_conv_out, mconv_w, mconv_b, w_q, w_k, w_v, w_if, b_if, mh_ln_g, skip, w_m_down, w_o, ln1_g, ln1_b, ln2_g, ln2_b, w_router, router_bias, w_e1, w_e3, w_e2, w_s1, w_s3, w_s2):
    bsz, seq, d = x_prompt.shape
    dbsz, dseq, _ = x_sample.shape
    depth = w_in.shape[0]
    n_meta = meta_tokens.shape[0]
    dc = conv_w.shape[2]
    di = mconv_w.shape[2]
    n_heads = state_m.shape[2]
    hd = di // n_heads
    alpha = (2 * depth) ** 0.25
    m0 = bsz * seq
    s0 = m0 + bsz * n_meta
    t_real = s0 + dbsz * dseq
    t_pad = _round_up(t_real, ROW_TILE)
    assert dc == d and di % dc == 0 and m0 % n_meta == 0 and s0 % (SUBLANES * dseq) == 0
    assert seq % min(SEQ_TILE, seq) == 0 and dbsz % SUBLANES == 0 and 2 * n_heads <= SUBLANES
    row = lambda a: a.reshape(1, -1)

    def flat(main, meta, samp):
        out = lax.dynamic_update_slice(main, meta, (m0, 0))
        return lax.dynamic_update_slice(out, samp, (s0, 0))

    x_all = jnp.concatenate([x_prompt.reshape(m0, d), jnp.tile(meta_tokens, (bsz, 1)), x_sample.reshape(dbsz * dseq, d),
                             jnp.zeros((t_pad - t_real, d), F32)], axis=0)
    (x,) = _rowwise(_ln_body, t_pad, [(x_all, d, 0)], [row(ln_in_g), row(ln_in_b)], [(d, F32)], "ln_in")

    xm_blk = 2 * dc // di
    og_blk = (2 * dc + di) // di
    ga_blk = (2 * dc + 2 * di) // d
    outs = {k: [] for k in ("conv_p", "mconv_p", "c_p", "n_p", "m_p", "conv_s", "mconv_s", "n_s", "m_s")}
    pdims, sdims = (bsz, seq, n_meta, t_pad), (s0, dbsz, dseq)
    w_in_b, w_co_b, w_md_b, w_o_b = (w.astype(BF16) for w in (w_in, w_conv_out, w_m_down, w_o))
    n_exp = w_e1.shape[1]
    we1, we3, we2 = (w.reshape((depth * n_exp,) + w.shape[2:]) for w in (w_e1, w_e3, w_e2))
    hist_c = state_conv.reshape((depth * dbsz,) + state_conv.shape[2:])
    hist_m = state_mconv.reshape((depth * dbsz,) + state_mconv.shape[2:])
    c0_all = state_C.reshape(depth * dbsz, n_heads, hd, hd)
    c_s = None
    for l in range(depth):
        proj = _in_proj(x, w_in_b, l)
        cw = (conv_w[l], row(conv_b[l]))
        cl = (row(conv_ln_g[l]), row(conv_ln_b[l]))
        wco = _Layer(w_co_b, l)
        a_main, a_meta, conv_p = _conv_prompt(proj, (0, 1), *cw, cl, _post_ln_silu, pdims, "conv_prompt", wco)
        a_samp, conv_s = _conv_sample(proj, (0, 1), hist_c, l, *cw, cl, _post_ln_silu, sdims, "conv_sample", wco)
        ya = flat(a_main, a_meta, a_samp)
        mw = (mconv_w[l], row(mconv_b[l]))
        cm_main, cm_meta, mconv_p = _conv_prompt(proj, (xm_blk,), *mw, (), _silu, pdims, "mconv_prompt")
        cm_samp, mconv_s = _conv_sample(proj, (xm_blk,), hist_m, l, *mw, (), _silu, sdims, "mconv_sample")
        cm = flat(cm_main, cm_meta, cm_samp)

        wif = jnp.pad(w_if[l], ((0, 0), (0, LANES - 2 * n_heads))).astype(BF16)
        wift = jnp.pad(w_if[l].T, ((0, SUBLANES - 2 * n_heads), (0, 0))).astype(BF16)
        bc = jnp.pad(b_if[l], (0, LANES - 2 * n_heads)).reshape(1, LANES)
        br = jnp.pad(b_if[l], (0, SUBLANES - 2 * n_heads)).reshape(SUBLANES, 1)
        q, k, v, gcol, grow = _qkv(cm, proj, xm_blk, *[w[l].reshape(-1, w.shape[-1]).T for w in (w_q, w_k, w_v)],
                                   wif, wift, bc, br)
        grow_meta = grow[:, m0:s0].reshape(SUBLANES, bsz, n_meta).transpose(1, 0, 2)
        grow_samp = grow[:, s0:t_real].reshape(SUBLANES, dbsz, dseq).transpose(1, 0, 2)
        hh_meta, hh_main, c_p, n_p, m_p = _mlstm_prompt(q, k, v, gcol, grow, grow_meta, pdims, n_heads)
        hh_samp, c_s, n_s, m_s = _mlstm_sample(
            q, k, v, gcol, grow_samp, c0_all, c_s, l, state_n[l],
            jnp.broadcast_to(state_m[l][:, :, None], (dbsz, n_heads, LANES)), sdims, n_heads)
        (yb,) = _rowwise(_down_body, t_pad, [(flat(hh_main, hh_meta, hh_samp), di, 0), (cm, di, 0), (proj, di, og_blk)],
                         [row(mh_ln_g[l]), row(skip[l]), _Layer(w_md_b, l)], [(d, F32)], "down")
        (h1,) = _rowwise(functools.partial(_merge_body, alpha), t_pad,
                         [(ya, d, 0), (yb, d, 0), (proj, d, ga_blk), (proj, d, ga_blk + 1), (x, d, 0)],
                         [_Layer(w_o_b, l), row(ln1_g[l]), row(ln1_b[l])], [(d, F32)], "merge")
        x = _moe(alpha, h1, w_router[l].T.astype(BF16), router_bias[l].reshape(-1, 1), we1, we3, we2, l,
                 w_s1[l].astype(BF16), w_s3[l].astype(BF16), w_s2[l].astype(BF16), row(ln2_g[l]), row(ln2_b[l]))

        outs["conv_p"].append(conv_p)
        outs["mconv_p"].append(mconv_p)
        outs["c_p"].append(c_p)
        outs["n_p"].append(n_p)
        outs["m_p"].append(m_p[:, :, 0])
        outs["conv_s"].append(conv_s)
        outs["mconv_s"].append(mconv_s)
        outs["n_s"].append(n_s)
        outs["m_s"].append(m_s[:, :, 0])

    y_prompt = x[:m0].reshape(bsz, seq, d)
    y_sample = x[s0:t_real].reshape(dbsz, dseq, d)
    st = {k: jnp.stack(v) for k, v in outs.items()}
    return (y_prompt, y_sample, st["conv_p"], st["mconv_p"], st["c_p"], st["n_p"], st["m_p"],
            st["conv_s"], st["mconv_s"], c_s.reshape(depth, dbsz, n_heads, hd, hd), st["n_s"], st["m_s"])
```
